```python
import math
import jax
import jax.numpy as jnp
from jax import lax
import numpy as np

D_MODEL = 2048
BATCH = 2
SEQ = 4096
DEPTH = 4

GRID_W = 64
CTX_LEN = 256
N_MOD = 9
FFN_RES = 0.5
D_FF = 5632
RMS_EPS = 1e-6
LN_EPS = 1e-5

D_MIX = D_MODEL
GROUP_W = D_MIX // 4

N_HEADS = 8
N_KV_HEADS = 2
HEAD_DIM = GROUP_W // N_HEADS
WINDOW = 128
BLOCK = 128
ROPE_BASE = 10000.0
NEG_INF = -1e30

SSM_H = 16
SSM_G = GROUP_W // SSM_H
SSM_P = 64
DT_MIN = 1e-3
DT_MAX = 1e-1
LAMBDA_RE_MAX = -1e-4

CONV_K = 31

FFT_HEADS = 4

KV_W = N_KV_HEADS * HEAD_DIM
K_OFF = 0
V_OFF = K_OFF + KV_W
SSM_OFF = V_OFF + KV_W
Q_OFF = SSM_OFF + GROUP_W
CONV_OFF = Q_OFF + GROUP_W
FFT_OFF = CONV_OFF + 2 * GROUP_W
D_IN = FFT_OFF + GROUP_W
CTX_COLS = Q_OFF

kernel_name = 'hybrid_parallel_group_diffusion_block'


def _rms_norm(x, g):
    xf = x.astype(jnp.float32)
    y = xf * lax.rsqrt(jnp.mean(xf * xf, axis=-1, keepdims=True) + RMS_EPS)
    return (y * g.astype(jnp.float32)).astype(x.dtype)


def _layer_norm(x, g, b):
    xf = x.astype(jnp.float32)
    mu = jnp.mean(xf, axis=-1, keepdims=True)
    var = jnp.mean(jnp.square(xf - mu), axis=-1, keepdims=True)
    y = (xf - mu) * lax.rsqrt(var + LN_EPS)
    return (y * g.astype(jnp.float32) + b.astype(jnp.float32)).astype(x.dtype)


def _swiglu(h, wi, wo):
    gt, up = jnp.split(h @ wi, 2, axis=-1)
    return (jax.nn.silu(gt) * up) @ wo


def _ffn_sublayer(s, mod, k, g_pre, g_post, wi, wo):
    shift, scale, gate = mod[3 * k], mod[3 * k + 1], mod[3 * k + 2]
    h = _rms_norm(s, g_pre) * (1 + scale) + shift
    return s + FFN_RES * gate * _rms_norm(_swiglu(h, wi, wo), g_post)


def _rope_1d(x, pos):
    half = x.shape[-1] // 2
    inv = ROPE_BASE ** (-jnp.arange(half, dtype=jnp.float32) / half)
    ang = pos.astype(jnp.float32)[:, None] * inv[None, :]
    cos = jnp.cos(ang)[:, None, :]
    sin = jnp.sin(ang)[:, None, :]
    xf = x.astype(jnp.float32)
    x1, x2 = xf[..., :half], xf[..., half:]
    return jnp.concatenate([x1 * cos - x2 * sin, x2 * cos + x1 * sin], axis=-1).astype(x.dtype)


def _axial_rope(x, row, col):
    r = x.shape[-1] // 2
    return jnp.concatenate([_rope_1d(x[..., :r], row), _rope_1d(x[..., r:], col)], axis=-1)


def _window_attention(q, k, v, kc, vc, sink):
    bsz, L = q.shape[0], q.shape[1]
    C = kc.shape[1]
    nb = L // BLOCK
    grp = N_HEADS // N_KV_HEADS
    scale = HEAD_DIM ** -0.5
    qb = q.reshape(bsz, nb, BLOCK, N_KV_HEADS, grp, HEAD_DIM)

    def band(t):
        tp = jnp.pad(t, ((0, 0), (BLOCK, BLOCK), (0, 0), (0, 0)))
        tp = tp.reshape(bsz, nb + 2, BLOCK, N_KV_HEADS, HEAD_DIM)
        return jnp.concatenate([tp[:, :-2], tp[:, 1:-1], tp[:, 2:]], axis=2)

    kb, vb = band(k), band(v)
    s_loc = jnp.einsum('bnqkgd,bnskd->bnkgqs', qb, kb, preferred_element_type=jnp.float32) * scale
    qpos = jnp.arange(nb)[:, None, None] * BLOCK + jnp.arange(BLOCK)[None, :, None]
    kpos = jnp.arange(nb)[:, None, None] * BLOCK - BLOCK + jnp.arange(3 * BLOCK)[None, None, :]
    valid = (jnp.abs(qpos - kpos) <= WINDOW) & (kpos >= 0) & (kpos < L)
    s_loc = jnp.where(valid[None, :, None, None], s_loc, NEG_INF)
    s_ctx = jnp.einsum('bnqkgd,bckd->bnkgqc', qb, kc, preferred_element_type=jnp.float32) * scale
    s_sink = jnp.broadcast_to(
        sink.astype(jnp.float32).reshape(N_KV_HEADS, grp)[None, None, :, :, None, None],
        s_loc.shape[:-1] + (1,))
    p = jax.nn.softmax(jnp.concatenate([s_loc, s_ctx, s_sink], axis=-1), axis=-1)
    n_loc = 3 * BLOCK
    p_loc = p[..., :n_loc].astype(vb.dtype)
    p_ctx = p[..., n_loc:n_loc + C].astype(vc.dtype)
    o = (jnp.einsum('bnkgqs,bnskd->bnqkgd', p_loc, vb)
         + jnp.einsum('bnkgqc,bckd->bnqkgd', p_ctx, vc))
    return o.reshape(bsz, L, N_HEADS * HEAD_DIM)


def _ctx_attention(qc, kc, vc, sink):
    bsz, C = qc.shape[0], qc.shape[1]
    grp = N_HEADS // N_KV_HEADS
    qg = qc.reshape(bsz, C, N_KV_HEADS, grp, HEAD_DIM)
    s = jnp.einsum('bqkgd,bskd->bkgqs', qg, kc, preferred_element_type=jnp.float32) * HEAD_DIM ** -0.5
    s_sink = jnp.broadcast_to(
        sink.astype(jnp.float32).reshape(N_KV_HEADS, grp)[None, :, :, None, None], s.shape[:-1] + (1,))
    p = jax.nn.softmax(jnp.concatenate([s, s_sink], axis=-1), axis=-1)[..., :C].astype(vc.dtype)
    o = jnp.einsum('bkgqs,bskd->bqkgd', p, vc)
    return o.reshape(bsz, C, N_HEADS * HEAD_DIM)


def _ssm_discretize(lam_re, lam_im, log_dt, b_re, b_im):
    lr = jnp.minimum(lam_re.astype(jnp.float32), LAMBDA_RE_MAX)
    li = lam_im.astype(jnp.float32)
    dt = jnp.exp(log_dt.astype(jnp.float32))[:, None]
    mag = jnp.exp(lr * dt)
    a_re = mag * jnp.cos(li * dt)
    a_im = mag * jnp.sin(li * dt)
    den = lr * lr + li * li
    nr = a_re - 1.0
    coef_re = ((nr * lr + a_im * li) / den)[..., None]
    coef_im = ((a_im * lr - nr * li) / den)[..., None]
    br = b_re.astype(jnp.float32)
    bi = b_im.astype(jnp.float32)
    return a_re, a_im, coef_re * br - coef_im * bi, coef_re * bi + coef_im * br


def _scan_combine(e1, e2):
    a1r, a1i, b1r, b1i = e1
    a2r, a2i, b2r, b2i = e2
    return (a2r * a1r - a2i * a1i, a2r * a1i + a2i * a1r,
            a2r * b1r - a2i * b1i + b2r, a2r * b1i + a2i * b1r + b2i)


def _complex_scan(a_re, a_im, bu_re, bu_im, h0_re, h0_im, reverse):
    if h0_re is not None:
        first = -1 if reverse else 0
        bu_re = bu_re.at[:, first].add(a_re * h0_re - a_im * h0_im)
        bu_im = bu_im.at[:, first].add(a_re * h0_im + a_im * h0_re)
    ar = jnp.broadcast_to(a_re, bu_re.shape)
    ai = jnp.broadcast_to(a_im, bu_im.shape)
    _, _, h_re, h_im = lax.associative_scan(_scan_combine, (ar, ai, bu_re, bu_im), reverse=reverse, axis=1)
    return h_re, h_im


def _ssm_drive(u, bb_re, bb_im):
    return jnp.einsum('blgh,gph->blgp', u, bb_re), jnp.einsum('blgh,gph->blgp', u, bb_im)


def _ssm_readout(h_re, h_im, c_re, c_im):
    return jnp.einsum('blgp,ghp->blgh', h_re, c_re) - jnp.einsum('blgp,ghp->blgh', h_im, c_im)


def _ssm_glu(y, w, b):
    y = jax.nn.gelu(y)
    return y * jax.nn.sigmoid(y @ w.astype(jnp.float32) + b.astype(jnp.float32))


def _ssm_branch(u, uc, need_ctx, lam_re, lam_im, log_dt, b_re, b_im, c_re, c_im, d_skip, glu_w, glu_b):
    bsz, L = u.shape[0], u.shape[1]
    C = uc.shape[1]
    ul = u.astype(jnp.float32).reshape(bsz, L, SSM_G, SSM_H)
    ucg = uc.astype(jnp.float32).reshape(bsz, C, SSM_G, SSM_H)
    dsk = d_skip.astype(jnp.float32).reshape(SSM_G, SSM_H)
    y_l = dsk * ul
    y_c = dsk * ucg if need_ctx else None
    for dr, reverse in enumerate((False, True)):
        a_re, a_im, bb_re, bb_im = _ssm_discretize(lam_re[dr], lam_im[dr], log_dt[dr], b_re[dr], b_im[dr])
        cr = c_re[dr].astype(jnp.float32)
        ci = c_im[dr].astype(jnp.float32)
        bc_re, bc_im = _ssm_drive(ucg, bb_re, bb_im)
        hc_re, hc_im = _complex_scan(a_re, a_im, bc_re, bc_im, None, None, reverse)
        fin = 0 if reverse else -1
        bl_re, bl_im = _ssm_drive(ul, bb_re, bb_im)
        hl_re, hl_im = _complex_scan(a_re, a_im, bl_re, bl_im, hc_re[:, fin], hc_im[:, fin], reverse)
        y_l = y_l + _ssm_readout(hl_re, hl_im, cr, ci)
        if need_ctx:
            y_c = y_c + _ssm_readout(hc_re, hc_im, cr, ci)
    out_l = _ssm_glu(y_l.reshape(bsz, L, GROUP_W), glu_w, glu_b).astype(u.dtype)
    out_c = _ssm_glu(y_c.reshape(bsz, C, GROUP_W), glu_w, glu_b).astype(uc.dtype) if need_ctx else None
    return out_l, out_c


def _conv_branch(p, conv_w, conv_b, ln_g, ln_b):
    val, gt = jnp.split(p, 2, axis=-1)
    h = val * jax.nn.sigmoid(gt)
    h = lax.conv_general_dilated(
        h, conv_w.astype(h.dtype)[:, None, :], window_strides=(1,),
        padding=[(CONV_K // 2, CONV_K // 2)], dimension_numbers=('NWC', 'WIO', 'NWC'),
        feature_group_count=GROUP_W) + conv_b
    return jax.nn.silu(_layer_norm(h, ln_g, ln_b))


def _fourier_branch(p):
    bsz, L = p.shape[0], p.shape[1]
    f = p.astype(jnp.float32).reshape(bsz, L, FFT_HEADS, GROUP_W // FFT_HEADS)
    f = jnp.fft.fft2(f, axes=(1, 3), norm='ortho').real
    return f.reshape(bsz, L, GROUP_W).astype(p.dtype)


def _mixer(h, hc, row, col, need_ctx, w_in, w_out, sink, lam_re, lam_im, log_dt, b_re, b_im,
           c_re, c_im, d_skip, glu_w, glu_b, conv_w, conv_b, ln_g, ln_b):
    bsz, L = h.shape[0], h.shape[1]
    C = hc.shape[1]
    p = h @ w_in
    pc = hc @ (w_in if need_ctx else w_in[:, :CTX_COLS])
    q = _axial_rope(p[..., Q_OFF:CONV_OFF].reshape(bsz, L, N_HEADS, HEAD_DIM), row, col)
    k = _axial_rope(p[..., K_OFF:V_OFF].reshape(bsz, L, N_KV_HEADS, HEAD_DIM), row, col)
    v = p[..., V_OFF:SSM_OFF].reshape(bsz, L, N_KV_HEADS, HEAD_DIM)
    kc = pc[..., K_OFF:V_OFF].reshape(bsz, C, N_KV_HEADS, HEAD_DIM)
    vc = pc[..., V_OFF:SSM_OFF].reshape(bsz, C, N_KV_HEADS, HEAD_DIM)
    att = _window_attention(q, k, v, kc, vc, sink)
    ssm_l, ssm_c = _ssm_branch(p[..., SSM_OFF:Q_OFF], pc[..., SSM_OFF:Q_OFF], need_ctx, lam_re, lam_im,
                               log_dt, b_re, b_im, c_re, c_im, d_skip, glu_w, glu_b)
    conv_l = _conv_branch(p[..., CONV_OFF:FFT_OFF], conv_w, conv_b, ln_g, ln_b)
    fft_l = _fourier_branch(p[..., FFT_OFF:D_IN])
    out = jnp.concatenate([att, ssm_l, conv_l, fft_l], axis=-1) @ w_out
    if not need_ctx:
        return out, None
    qc = pc[..., Q_OFF:CONV_OFF].reshape(bsz, C, N_HEADS, HEAD_DIM)
    att_c = _ctx_attention(qc, kc, vc, sink)
    conv_c = _conv_branch(pc[..., CONV_OFF:FFT_OFF], conv_w, conv_b, ln_g, ln_b)
    fft_c = _fourier_branch(pc[..., FFT_OFF:D_IN])
    out_c = jnp.concatenate([att_c, ssm_c, conv_c, fft_c], axis=-1) @ w_out
    return out, out_c


def setup_inputs(seed: int = 0) -> dict:
    key = jax.random.key(seed)
    ks = jax.random.split(key, 26)
    f32 = jnp.float32
    D = D_MODEL

    def nrm(k, shape, s):
        return s * jax.random.normal(k, shape, f32)

    ssm_shape = (DEPTH, 2, SSM_G, SSM_P)
    lam_im0 = math.pi * jnp.arange(SSM_P, dtype=f32)
    return {
        'x': nrm(ks[0], (BATCH, SEQ, D), 1.0),
        'c': nrm(ks[1], (BATCH, D), 1.0),
        'ctx': nrm(ks[2], (BATCH, CTX_LEN, D), 1.0),
        'c_ctx': nrm(ks[3], (D,), 1.0),
        'w_mod': nrm(ks[4], (DEPTH, D, N_MOD * D), 0.5 * D ** -0.5),
        'b_mod': nrm(ks[5], (DEPTH, N_MOD * D), 0.02),
        'norm_g': 1.0 + nrm(ks[6], (DEPTH, 6, D), 0.05),
        'ffn_wi': nrm(ks[7], (DEPTH, 2, D, 2 * D_FF), D ** -0.5),
        'ffn_wo': nrm(ks[8], (DEPTH, 2, D_FF, D), D_FF ** -0.5),
        'w_in': nrm(ks[9], (DEPTH, D, D_IN), D ** -0.5),
        'w_out': nrm(ks[10], (DEPTH, D_MIX, D), D_MIX ** -0.5),
        'attn_sink': nrm(ks[11], (DEPTH, N_HEADS), 0.5),
        'ssm_lam_re': -0.5 + nrm(ks[12], ssm_shape, 0.01),
        'ssm_lam_im': lam_im0 + nrm(ks[13], ssm_shape, 0.01),
        'ssm_log_dt': jax.random.uniform(ks[14], (DEPTH, 2, SSM_G), f32, math.log(DT_MIN), math.log(DT_MAX)),
        'ssm_b_re': nrm(ks[15], (DEPTH, 2, SSM_G, SSM_P, SSM_H), (2 * SSM_H) ** -0.5),
        'ssm_b_im': nrm(ks[16], (DEPTH, 2, SSM_G, SSM_P, SSM_H), (2 * SSM_H) ** -0.5),
        'ssm_c_re': nrm(ks[17], (DEPTH, 2, SSM_G, SSM_H, SSM_P), 0.5 ** 0.5),
        'ssm_c_im': nrm(ks[18], (DEPTH, 2, SSM_G, SSM_H, SSM_P), 0.5 ** 0.5),
        'ssm_d': nrm(ks[19], (DEPTH, GROUP_W), 1.0),
        'ssm_glu_w': nrm(ks[20], (DEPTH, GROUP_W, GROUP_W), GROUP_W ** -0.5),
        'ssm_glu_b': nrm(ks[21], (DEPTH, GROUP_W), 0.02),
        'conv_w': nrm(ks[22], (DEPTH, CONV_K, GROUP_W), CONV_K ** -0.5),
        'conv_b': nrm(ks[23], (DEPTH, GROUP_W), 0.02),
        'conv_ln_g': 1.0 + nrm(ks[24], (DEPTH, GROUP_W), 0.05),
        'conv_ln_b': nrm(ks[25], (DEPTH, GROUP_W), 0.02),
    }


def reference(x, c, ctx, c_ctx, w_mod, b_mod, norm_g, ffn_wi, ffn_wo, w_in, w_out, attn_sink,
              ssm_lam_re, ssm_lam_im, ssm_log_dt, ssm_b_re, ssm_b_im, ssm_c_re, ssm_c_im, ssm_d,
              ssm_glu_w, ssm_glu_b, conv_w, conv_b, conv_ln_g, conv_ln_b):
    bsz, L = x.shape[0], x.shape[1]
    rows = L // GRID_W
    row = jnp.repeat(jnp.arange(rows), GRID_W)
    col = jnp.tile(jnp.arange(GRID_W), rows)
    sc = jax.nn.silu(c)
    scc = jax.nn.silu(c_ctx)
    for l in range(DEPTH):
        need_ctx = l < DEPTH - 1
        g = norm_g[l]
        m = (sc @ w_mod[l] + b_mod[l]).reshape(bsz, N_MOD, D_MODEL).transpose(1, 0, 2)[:, :, None, :]
        mc = (scc @ w_mod[l] + b_mod[l]).reshape(N_MOD, D_MODEL)[:, None, None, :]
        x = _ffn_sublayer(x, m, 0, g[0], g[1], ffn_wi[l, 0], ffn_wo[l, 0])
        ctx = _ffn_sublayer(ctx, mc, 0, g[0], g[1], ffn_wi[l, 0], ffn_wo[l, 0])
        h = _rms_norm(x, g[2]) * (1 + m[4]) + m[3]
        hc = _rms_norm(ctx, g[2]) * (1 + mc[4]) + mc[3]
        out, out_c = _mixer(h, hc, row, col, need_ctx, w_in[l], w_out[l], attn_sink[l],
                            ssm_lam_re[l], ssm_lam_im[l], ssm_log_dt[l], ssm_b_re[l], ssm_b_im[l],
                            ssm_c_re[l], ssm_c_im[l], ssm_d[l], ssm_glu_w[l], ssm_glu_b[l],
                            conv_w[l], conv_b[l], conv_ln_g[l], conv_ln_b[l])
        x = x + m[5] * _rms_norm(out, g[3])
        x = _ffn_sublayer(x, m, 2, g[4], g[5], ffn_wi[l, 1], ffn_wo[l, 1])
        if need_ctx:
            ctx = ctx + mc[5] * _rms_norm(out_c, g[3])
            ctx = _ffn_sublayer(ctx, mc, 2, g[4], g[5], ffn_wi[l, 1], ffn_wo[l, 1])
    return x
```

```python
import functools
import math

import jax
import jax.numpy as jnp
from jax import lax
from jax.experimental import pallas as pl
from jax.experimental.pallas import tpu as pltpu

F32 = jnp.float32
BF16 = jnp.bfloat16

D_MODEL = 2048
DEPTH = 4
GRID_W = 64
N_MOD = 9
FFN_RES = 0.5
D_FF = 5632
RMS_EPS = 1e-6
LN_EPS = 1e-5
GROUP_W = 512
N_HEADS = 8
N_KV_HEADS = 2
HEAD_DIM = 64
WINDOW = 128
BLOCK = 128
ROPE_BASE = 10000.0
NEG_INF = -1e30
SSM_H = 16
SSM_G = 32
SSM_P = 64
LAMBDA_RE_MAX = -1e-4
CONV_K = 31
FFT_HEADS = 4
KV_W = N_KV_HEADS * HEAD_DIM
K_OFF = 0
V_OFF = K_OFF + KV_W
SSM_OFF = V_OFF + KV_W
Q_OFF = SSM_OFF + GROUP_W
CONV_OFF = Q_OFF + GROUP_W
FFT_OFF = CONV_OFF + 2 * GROUP_W
D_IN = FFT_OFF + GROUP_W
CTX_COLS = Q_OFF

VMEM_LIMIT_BYTES = 56 * 1024 * 1024
MOD_ROWS = 8
MOD_TN = 1024
FFN_TM = 512
FFN_TF = 512


def _mod_kernel(c_ref, w_ref, b_ref, o_ref):
    c = c_ref[...]
    sc = (c * jax.nn.sigmoid(c)).astype(BF16)
    o_ref[0] = jnp.dot(sc, w_ref[0].astype(BF16), preferred_element_type=F32) + b_ref[0]


def _modulation(cond, w_mod, b_mod):
    depth, d, n = w_mod.shape
    return pl.pallas_call(
        _mod_kernel,
        grid=(depth, n // MOD_TN),
        in_specs=[
            pl.BlockSpec((MOD_ROWS, d), lambda l, j: (0, 0)),
            pl.BlockSpec((1, d, MOD_TN), lambda l, j: (l, 0, j)),
            pl.BlockSpec((1, 1, MOD_TN), lambda l, j: (l, 0, j)),
        ],
        out_specs=pl.BlockSpec((1, MOD_ROWS, MOD_TN), lambda l, j: (l, 0, j)),
        out_shape=jax.ShapeDtypeStruct((depth, MOD_ROWS, n), F32),
        compiler_params=pltpu.CompilerParams(
            dimension_semantics=("arbitrary", "arbitrary"), vmem_limit_bytes=VMEM_LIMIT_BYTES),
    )(cond, w_mod, b_mod.reshape(depth, 1, n))


def _ffn_kernel(s_ref, mod_ref, gpre_ref, gpost_ref, wg_ref, wu_ref, wo_ref, o_ref, h_ref):
    j = pl.program_id(1)

    @pl.when(j == 0)
    def _():
        s = s_ref[...]
        y = s * lax.rsqrt(jnp.mean(s * s, axis=-1, keepdims=True) + RMS_EPS) * gpre_ref[...]
        h = y * (1.0 + mod_ref[0, 0, 1:2, :]) + mod_ref[0, 0, 0:1, :]
        h_ref[...] = h.astype(BF16)
        o_ref[...] = jnp.zeros_like(o_ref)

    h = h_ref[...]
    gt = jnp.dot(h, wg_ref[...].astype(BF16), preferred_element_type=F32)
    up = jnp.dot(h, wu_ref[...].astype(BF16), preferred_element_type=F32)
    a = (gt * jax.nn.sigmoid(gt) * up).astype(BF16)
    o_ref[...] += jnp.dot(a, wo_ref[...].astype(BF16), preferred_element_type=F32)

    @pl.when(j == pl.num_programs(1) - 1)
    def _():
        acc = o_ref[...]
        y = acc * lax.rsqrt(jnp.mean(acc * acc, axis=-1, keepdims=True) + RMS_EPS) * gpost_ref[...]
        o_ref[...] = s_ref[...] + (FFN_RES * mod_ref[0, 0, 2:3, :]) * y


def _ffn_sublayer(s, mod, g_pre, g_post, wi, wo, rows_per_mod, n_rows):
    d = s.shape[1]
    n_ff = wo.shape[0]
    tiles_per_mod = rows_per_mod // FFN_TM
    nj = n_ff // FFN_TF
    return pl.pallas_call(
        _ffn_kernel,
        grid=(n_rows // FFN_TM, nj),
        in_specs=[
            pl.BlockSpec((FFN_TM, d), lambda i, j: (i, 0)),
            pl.BlockSpec((1, 1, 3, d), lambda i, j: (jnp.minimum(i // tiles_per_mod, 2), 0, 0, 0)),
            pl.BlockSpec((1, d), lambda i, j: (0, 0)),
            pl.BlockSpec((1, d), lambda i, j: (0, 0)),
            pl.BlockSpec((d, FFN_TF), lambda i, j: (0, j)),
            pl.BlockSpec((d, FFN_TF), lambda i, j: (0, j + nj)),
            pl.BlockSpec((FFN_TF, d), lambda i, j: (j, 0)),
        ],
        out_specs=pl.BlockSpec((FFN_TM, d), lambda i, j: (i, 0)),
        out_shape=jax.ShapeDtypeStruct((n_rows, d), F32),
        scratch_shapes=[pltpu.VMEM((FFN_TM, d), BF16)],
        compiler_params=pltpu.CompilerParams(
            dimension_semantics=("arbitrary", "arbitrary"), vmem_limit_bytes=VMEM_LIMIT_BYTES),
    )(s, mod, g_pre.reshape(1, d), g_post.reshape(1, d), wi, wi, wo)


def _rms_norm(x, g):
    y = x * lax.rsqrt(jnp.mean(x * x, axis=-1, keepdims=True) + RMS_EPS)
    return y * g


def _layer_norm(x, g, b):
    mu = jnp.mean(x, axis=-1, keepdims=True)
    var = jnp.mean(jnp.square(x - mu), axis=-1, keepdims=True)
    return (x - mu) * lax.rsqrt(var + LN_EPS) * g + b


def _rope_1d(x, pos):
    half = x.shape[-1] // 2
    inv = ROPE_BASE ** (-jnp.arange(half, dtype=F32) / half)
    ang = pos.astype(F32)[:, None] * inv[None, :]
    cos = jnp.cos(ang)[:, None, :]
    sin = jnp.sin(ang)[:, None, :]
    x1, x2 = x[..., :half], x[..., half:]
    return jnp.concatenate([x1 * cos - x2 * sin, x2 * cos + x1 * sin], axis=-1)


def _axial_rope(x, row, col):
    r = x.shape[-1] // 2
    return jnp.concatenate([_rope_1d(x[..., :r], row), _rope_1d(x[..., r:], col)], axis=-1)


def _window_attention(q, k, v, kc, vc, sink):
    bsz, L = q.shape[0], q.shape[1]
    C = kc.shape[1]
    nb = L // BLOCK
    grp = N_HEADS // N_KV_HEADS
    scale = HEAD_DIM ** -0.5
    qb = q.reshape(bsz, nb, BLOCK, N_KV_HEADS, grp, HEAD_DIM)

    def band(t):
        tp = jnp.pad(t, ((0, 0), (BLOCK, BLOCK), (0, 0), (0, 0)))
        tp = tp.reshape(bsz, nb + 2, BLOCK, N_KV_HEADS, HEAD_DIM)
        return jnp.concatenate([tp[:, :-2], tp[:, 1:-1], tp[:, 2:]], axis=2)

    kb, vb = band(k), band(v)
    s_loc = jnp.einsum('bnqkgd,bnskd->bnkgqs', qb, kb, preferred_element_type=F32) * scale
    qpos = jnp.arange(nb)[:, None, None] * BLOCK + jnp.arange(BLOCK)[None, :, None]
    kpos = jnp.arange(nb)[:, None, None] * BLOCK - BLOCK + jnp.arange(3 * BLOCK)[None, None, :]
    valid = (jnp.abs(qpos - kpos) <= WINDOW) & (kpos >= 0) & (kpos < L)
    s_loc = jnp.where(valid[None, :, None, None], s_loc, NEG_INF)
    s_ctx = jnp.einsum('bnqkgd,bckd->bnkgqc', qb, kc, preferred_element_type=F32) * scale
    s_sink = jnp.broadcast_to(
        sink.reshape(N_KV_HEADS, grp)[None, None, :, :, None, None], s_loc.shape[:-1] + (1,))
    p = jax.nn.softmax(jnp.concatenate([s_loc, s_ctx, s_sink], axis=-1), axis=-1)
    n_loc = 3 * BLOCK
    o = (jnp.einsum('bnkgqs,bnskd->bnqkgd', p[..., :n_loc], vb)
         + jnp.einsum('bnkgqc,bckd->bnqkgd', p[..., n_loc:n_loc + C], vc))
    return o.reshape(bsz, L, N_HEADS * HEAD_DIM)


def _ctx_attention(qc, kc, vc, sink):
    bsz, C = qc.shape[0], qc.shape[1]
    grp = N_HEADS // N_KV_HEADS
    qg = qc.reshape(bsz, C, N_KV_HEADS, grp, HEAD_DIM)
    s = jnp.einsum('bqkgd,bskd->bkgqs', qg, kc, preferred_element_type=F32) * HEAD_DIM ** -0.5
    s_sink = jnp.broadcast_to(sink.reshape(N_KV_HEADS, grp)[None, :, :, None, None], s.shape[:-1] + (1,))
    p = jax.nn.softmax(jnp.concatenate([s, s_sink], axis=-1), axis=-1)[..., :C]
    o = jnp.einsum('bkgqs,bskd->bqkgd', p, vc)
    return o.reshape(bsz, C, N_HEADS * HEAD_DIM)


def _ssm_discretize(lam_re, lam_im, log_dt, b_re, b_im):
    lr = jnp.minimum(lam_re, LAMBDA_RE_MAX)
    li = lam_im
    dt = jnp.exp(log_dt)[:, None]
    mag = jnp.exp(lr * dt)
    a_re = mag * jnp.cos(li * dt)
    a_im = mag * jnp.sin(li * dt)
    den = lr * lr + li * li
    nr = a_re - 1.0
    coef_re = ((nr * lr + a_im * li) / den)[..., None]
    coef_im = ((a_im * lr - nr * li) / den)[..., None]
    return a_re, a_im, coef_re * b_re - coef_im * b_im, coef_re * b_im + coef_im * b_re


def _scan_combine(e1, e2):
    a1r, a1i, b1r, b1i = e1
    a2r, a2i, b2r, b2i = e2
    return (a2r * a1r - a2i * a1i, a2r * a1i + a2i * a1r,
            a2r * b1r - a2i * b1i + b2r, a2r * b1i + a2i * b1r + b2i)


def _complex_scan(a_re, a_im, bu_re, bu_im, h0_re, h0_im, reverse):
    if h0_re is not None:
        first = -1 if reverse else 0
        bu_re = bu_re.at[:, first].add(a_re * h0_re - a_im * h0_im)
        bu_im = bu_im.at[:, first].add(a_re * h0_im + a_im * h0_re)
    ar = jnp.broadcast_to(a_re, bu_re.shape)
    ai = jnp.broadcast_to(a_im, bu_im.shape)
    _, _, h_re, h_im = lax.associative_scan(_scan_combine, (ar, ai, bu_re, bu_im), reverse=reverse, axis=1)
    return h_re, h_im


def _ssm_glu(y, w, b):
    y = jax.nn.gelu(y)
    return y * jax.nn.sigmoid(y @ w + b)


def _ssm_branch(u, uc, need_ctx, lam_re, lam_im, log_dt, b_re, b_im, c_re, c_im, d_skip, glu_w, glu_b):
    bsz, L = u.shape[0], u.shape[1]
    C = uc.shape[1]
    ul = u.reshape(bsz, L, SSM_G, SSM_H)
    ucg = uc.reshape(bsz, C, SSM_G, SSM_H)
    dsk = d_skip.reshape(SSM_G, SSM_H)
    y_l = dsk * ul
    y_c = dsk * ucg if need_ctx else None
    for dr, reverse in enumerate((False, True)):
        a_re, a_im, bb_re, bb_im = _ssm_discretize(lam_re[dr], lam_im[dr], log_dt[dr], b_re[dr], b_im[dr])
        cr, ci = c_re[dr], c_im[dr]
        bc_re = jnp.einsum('blgh,gph->blgp', ucg, bb_re)
        bc_im = jnp.einsum('blgh,gph->blgp', ucg, bb_im)
        hc_re, hc_im = _complex_scan(a_re, a_im, bc_re, bc_im, None, None, reverse)
        fin = 0 if reverse else -1
        bl_re = jnp.einsum('blgh,gph->blgp', ul, bb_re)
        bl_im = jnp.einsum('blgh,gph->blgp', ul, bb_im)
        hl_re, hl_im = _complex_scan(a_re, a_im, bl_re, bl_im, hc_re[:, fin], hc_im[:, fin], reverse)
        y_l = y_l + jnp.einsum('blgp,ghp->blgh', hl_re, cr) - jnp.einsum('blgp,ghp->blgh', hl_im, ci)
        if need_ctx:
            y_c = y_c + jnp.einsum('blgp,ghp->blgh', hc_re, cr) - jnp.einsum('blgp,ghp->blgh', hc_im, ci)
    out_l = _ssm_glu(y_l.reshape(bsz, L, GROUP_W), glu_w, glu_b)
    out_c = _ssm_glu(y_c.reshape(bsz, C, GROUP_W), glu_w, glu_b) if need_ctx else None
    return out_l, out_c


def _conv_branch(p, conv_w, conv_b, ln_g, ln_b):
    val, gt = jnp.split(p, 2, axis=-1)
    h = val * jax.nn.sigmoid(gt)
    h = lax.conv_general_dilated(
        h, conv_w[:, None, :], window_strides=(1,),
        padding=[(CONV_K // 2, CONV_K // 2)], dimension_numbers=('NWC', 'WIO', 'NWC'),
        feature_group_count=GROUP_W) + conv_b
    return jax.nn.silu(_layer_norm(h, ln_g, ln_b))


def _fourier_branch(p):
    bsz, L = p.shape[0], p.shape[1]
    f = p.reshape(bsz, L, FFT_HEADS, GROUP_W // FFT_HEADS)
    f = jnp.fft.fft2(f, axes=(1, 3), norm='ortho').real
    return f.reshape(bsz, L, GROUP_W)


def _mixer(h, hc, row, col, need_ctx, w_in, w_out, sink, lam_re, lam_im, log_dt, b_re, b_im,
           c_re, c_im, d_skip, glu_w, glu_b, conv_w, conv_b, ln_g, ln_b):
    bsz, L = h.shape[0], h.shape[1]
    C = hc.shape[1]
    p = h @ w_in
    pc = hc @ (w_in if need_ctx else w_in[:, :CTX_COLS])
    q = _axial_rope(p[..., Q_OFF:CONV_OFF].reshape(bsz, L, N_HEADS, HEAD_DIM), row, col)
    k = _axial_rope(p[..., K_OFF:V_OFF].reshape(bsz, L, N_KV_HEADS, HEAD_DIM), row, col)
    v = p[..., V_OFF:SSM_OFF].reshape(bsz, L, N_KV_HEADS, HEAD_DIM)
    kc = pc[..., K_OFF:V_OFF].reshape(bsz, C, N_KV_HEADS, HEAD_DIM)
    vc = pc[..., V_OFF:SSM_OFF].reshape(bsz, C, N_KV_HEADS, HEAD_DIM)
    att = _window_attention(q, k, v, kc, vc, sink)
    ssm_l, ssm_c = _ssm_branch(p[..., SSM_OFF:Q_OFF], pc[..., SSM_OFF:Q_OFF], need_ctx, lam_re, lam_im,
                               log_dt, b_re, b_im, c_re, c_im, d_skip, glu_w, glu_b)
    conv_l = _conv_branch(p[..., CONV_OFF:FFT_OFF], conv_w, conv_b, ln_g, ln_b)
    fft_l = _fourier_branch(p[..., FFT_OFF:D_IN])
    out = jnp.concatenate([att, ssm_l, conv_l, fft_l], axis=-1) @ w_out
    if not need_ctx:
        return out, None
    qc = pc[..., Q_OFF:CONV_OFF].reshape(bsz, C, N_HEADS, HEAD_DIM)
    att_c = _ctx_attention(qc, kc, vc, sink)
    conv_c = _conv_branch(pc[..., CONV_OFF:FFT_OFF], conv_w, conv_b, ln_g, ln_b)
    fft_c = _fourier_branch(pc[..., FFT_OFF:D_IN])
    out_c = jnp.concatenate([att_c, ssm_c, conv_c, fft_c], axis=-1) @ w_out
    return out, out_c


def kernel(x, c, ctx, c_ctx, w_mod, b_mod, norm_g, ffn_wi, ffn_wo, w_in, w_out, attn_sink,
           ssm_lam_re, ssm_lam_im, ssm_log_dt, ssm_b_re, ssm_b_im, ssm_c_re, ssm_c_im, ssm_d,
           ssm_glu_w, ssm_glu_b, conv_w, conv_b, conv_ln_g, conv_ln_b):
    bsz, L, d = x.shape
    C = ctx.shape[1]
    n_lat = bsz * L
    n_all = n_lat + bsz * C
    rows = L // GRID_W
    row = jnp.repeat(jnp.arange(rows), GRID_W)
    col = jnp.tile(jnp.arange(GRID_W), rows)

    cond = jnp.concatenate([c, c_ctx[None, :], jnp.zeros((MOD_ROWS - bsz - 1, d), F32)], axis=0)
    mod_all = _modulation(cond, w_mod, b_mod)
    mod_all = mod_all[:, :bsz + 1].reshape(DEPTH, bsz + 1, 3, 3, d)

    s = jnp.concatenate([x.reshape(n_lat, d), ctx.reshape(bsz * C, d)], axis=0)
    for l in range(DEPTH):
        need_ctx = l < DEPTH - 1
        g = norm_g[l]
        m = mod_all[l]
        s = _ffn_sublayer(s, m[:, 0:1], g[0], g[1], ffn_wi[l, 0], ffn_wo[l, 0], L, n_all)
        xl = s[:n_lat].reshape(bsz, L, d)
        xc = s[n_lat:].reshape(bsz, C, d)
        ml = m[:bsz, 1][:, :, None, :]
        mc = m[bsz, 1]
        h = _rms_norm(xl, g[2]) * (1 + ml[:, 1]) + ml[:, 0]
        hc = _rms_norm(xc, g[2]) * (1 + mc[1]) + mc[0]
        out, out_c = _mixer(h, hc, row, col, need_ctx, w_in[l], w_out[l], attn_sink[l],
                            ssm_lam_re[l], ssm_lam_im[l], ssm_log_dt[l], ssm_b_re[l], ssm_b_im[l],
                            ssm_c_re[l], ssm_c_im[l], ssm_d[l], ssm_glu_w[l], ssm_glu_b[l],
                            conv_w[l], conv_b[l], conv_ln_g[l], conv_ln_b[l])
        xl = xl + ml[:, 2] * _rms_norm(out, g[3])
        if need_ctx:
            xc = xc + mc[2] * _rms_norm(out_c, g[3])
            s = jnp.concatenate([xl.reshape(n_lat, d), xc.reshape(bsz * C, d)], axis=0)
            s = _ffn_sublayer(s, m[:, 2:3], g[4], g[5], ffn_wi[l, 1], ffn_wo[l, 1], L, n_all)
        else:
            s = _ffn_sublayer(xl.reshape(n_lat, d), m[:, 2:3], g[4], g[5], ffn_wi[l, 1], ffn_wo[l, 1], L, n_lat)
    return s.reshape(bsz, L, d)
```

```python
import functools
import math

import jax
import jax.numpy as jnp
from jax import lax
from jax.experimental import pallas as pl
from jax.experimental.pallas import tpu as pltpu

F32 = jnp.float32
BF16 = jnp.bfloat16

D_MODEL = 2048
DEPTH = 4
GRID_W = 64
N_MOD = 9
FFN_RES = 0.5
D_FF = 5632
RMS_EPS = 1e-6
LN_EPS = 1e-5
GROUP_W = 512
N_HEADS = 8
N_KV_HEADS = 2
HEAD_DIM = 64
WINDOW = 128
BLOCK = 128
ROPE_BASE = 10000.0
NEG_INF = -1e30
SSM_H = 16
SSM_G = 32
SSM_P = 64
LAMBDA_RE_MAX = -1e-4
SSM_T = 16
SCAN_K = 11
CONV_K = 31
FFT_HEADS = 4
KV_W = N_KV_HEADS * HEAD_DIM
K_OFF = 0
V_OFF = K_OFF + KV_W
SSM_OFF = V_OFF + KV_W
Q_OFF = SSM_OFF + GROUP_W
CONV_OFF = Q_OFF + GROUP_W
FFT_OFF = CONV_OFF + 2 * GROUP_W
D_IN = FFT_OFF + GROUP_W
CTX_COLS = Q_OFF

VMEM_LIMIT_BYTES = 56 * 1024 * 1024
MOD_ROWS = 8
MOD_TN = 1024
FFN_TM = 512
FFN_TF = 512


def _mod_kernel(c_ref, w_ref, b_ref, o_ref):
    c = c_ref[...]
    sc = (c * jax.nn.sigmoid(c)).astype(BF16)
    o_ref[0] = jnp.dot(sc, w_ref[0].astype(BF16), preferred_element_type=F32) + b_ref[0]


def _modulation(cond, w_mod, b_mod):
    depth, d, n = w_mod.shape
    return pl.pallas_call(
        _mod_kernel,
        grid=(depth, n // MOD_TN),
        in_specs=[
            pl.BlockSpec((MOD_ROWS, d), lambda l, j: (0, 0)),
            pl.BlockSpec((1, d, MOD_TN), lambda l, j: (l, 0, j)),
            pl.BlockSpec((1, 1, MOD_TN), lambda l, j: (l, 0, j)),
        ],
        out_specs=pl.BlockSpec((1, MOD_ROWS, MOD_TN), lambda l, j: (l, 0, j)),
        out_shape=jax.ShapeDtypeStruct((depth, MOD_ROWS, n), F32),
        compiler_params=pltpu.CompilerParams(
            dimension_semantics=("arbitrary", "arbitrary"), vmem_limit_bytes=VMEM_LIMIT_BYTES),
    )(cond, w_mod, b_mod.reshape(depth, 1, n))


def _ffn_kernel(s_ref, mod_ref, gpre_ref, gpost_ref, wg_ref, wu_ref, wo_ref, o_ref, h_ref):
    j = pl.program_id(1)

    @pl.when(j == 0)
    def _():
        s = s_ref[...]
        y = s * lax.rsqrt(jnp.mean(s * s, axis=-1, keepdims=True) + RMS_EPS) * gpre_ref[...]
        h = y * (1.0 + mod_ref[0, 0, 1:2, :]) + mod_ref[0, 0, 0:1, :]
        h_ref[...] = h.astype(BF16)
        o_ref[...] = jnp.zeros_like(o_ref)

    h = h_ref[...]
    gt = jnp.dot(h, wg_ref[...].astype(BF16), preferred_element_type=F32)
    up = jnp.dot(h, wu_ref[...].astype(BF16), preferred_element_type=F32)
    a = (gt * jax.nn.sigmoid(gt) * up).astype(BF16)
    o_ref[...] += jnp.dot(a, wo_ref[...].astype(BF16), preferred_element_type=F32)

    @pl.when(j == pl.num_programs(1) - 1)
    def _():
        acc = o_ref[...]
        y = acc * lax.rsqrt(jnp.mean(acc * acc, axis=-1, keepdims=True) + RMS_EPS) * gpost_ref[...]
        o_ref[...] = s_ref[...] + (FFN_RES * mod_ref[0, 0, 2:3, :]) * y


def _ffn_sublayer(s, mod, g_pre, g_post, wi, wo, rows_per_mod, n_rows):
    d = s.shape[1]
    n_ff = wo.shape[0]
    tiles_per_mod = rows_per_mod // FFN_TM
    nj = n_ff // FFN_TF
    return pl.pallas_call(
        _ffn_kernel,
        grid=(n_rows // FFN_TM, nj),
        in_specs=[
            pl.BlockSpec((FFN_TM, d), lambda i, j: (i, 0)),
            pl.BlockSpec((1, 1, 3, d), lambda i, j: (jnp.minimum(i // tiles_per_mod, 2), 0, 0, 0)),
            pl.BlockSpec((1, d), lambda i, j: (0, 0)),
            pl.BlockSpec((1, d), lambda i, j: (0, 0)),
            pl.BlockSpec((d, FFN_TF), lambda i, j: (0, j)),
            pl.BlockSpec((d, FFN_TF), lambda i, j: (0, j + nj)),
            pl.BlockSpec((FFN_TF, d), lambda i, j: (j, 0)),
        ],
        out_specs=pl.BlockSpec((FFN_TM, d), lambda i, j: (i, 0)),
        out_shape=jax.ShapeDtypeStruct((n_rows, d), F32),
        scratch_shapes=[pltpu.VMEM((FFN_TM, d), BF16)],
        compiler_params=pltpu.CompilerParams(
            dimension_semantics=("arbitrary", "arbitrary"), vmem_limit_bytes=VMEM_LIMIT_BYTES),
    )(s, mod, g_pre.reshape(1, d), g_post.reshape(1, d), wi, wi, wo)


def _rms_norm(x, g):
    y = x * lax.rsqrt(jnp.mean(x * x, axis=-1, keepdims=True) + RMS_EPS)
    return y * g


def _layer_norm(x, g, b):
    mu = jnp.mean(x, axis=-1, keepdims=True)
    var = jnp.mean(jnp.square(x - mu), axis=-1, keepdims=True)
    return (x - mu) * lax.rsqrt(var + LN_EPS) * g + b


def _rope_1d(x, pos):
    half = x.shape[-1] // 2
    inv = ROPE_BASE ** (-jnp.arange(half, dtype=F32) / half)
    ang = pos.astype(F32)[:, None] * inv[None, :]
    cos = jnp.cos(ang)[:, None, :]
    sin = jnp.sin(ang)[:, None, :]
    x1, x2 = x[..., :half], x[..., half:]
    return jnp.concatenate([x1 * cos - x2 * sin, x2 * cos + x1 * sin], axis=-1)


def _axial_rope(x, row, col):
    r = x.shape[-1] // 2
    return jnp.concatenate([_rope_1d(x[..., :r], row), _rope_1d(x[..., r:], col)], axis=-1)


def _window_attention(q, k, v, kc, vc, sink):
    bsz, L = q.shape[0], q.shape[1]
    C = kc.shape[1]
    nb = L // BLOCK
    grp = N_HEADS // N_KV_HEADS
    scale = HEAD_DIM ** -0.5
    qb = q.reshape(bsz, nb, BLOCK, N_KV_HEADS, grp, HEAD_DIM)

    def band(t):
        tp = jnp.pad(t, ((0, 0), (BLOCK, BLOCK), (0, 0), (0, 0)))
        tp = tp.reshape(bsz, nb + 2, BLOCK, N_KV_HEADS, HEAD_DIM)
        return jnp.concatenate([tp[:, :-2], tp[:, 1:-1], tp[:, 2:]], axis=2)

    kb, vb = band(k), band(v)
    s_loc = jnp.einsum('bnqkgd,bnskd->bnkgqs', qb, kb, preferred_element_type=F32) * scale
    qpos = jnp.arange(nb)[:, None, None] * BLOCK + jnp.arange(BLOCK)[None, :, None]
    kpos = jnp.arange(nb)[:, None, None] * BLOCK - BLOCK + jnp.arange(3 * BLOCK)[None, None, :]
    valid = (jnp.abs(qpos - kpos) <= WINDOW) & (kpos >= 0) & (kpos < L)
    s_loc = jnp.where(valid[None, :, None, None], s_loc, NEG_INF)
    s_ctx = jnp.einsum('bnqkgd,bckd->bnkgqc', qb, kc, preferred_element_type=F32) * scale
    s_sink = jnp.broadcast_to(
        sink.reshape(N_KV_HEADS, grp)[None, None, :, :, None, None], s_loc.shape[:-1] + (1,))
    p = jax.nn.softmax(jnp.concatenate([s_loc, s_ctx, s_sink], axis=-1), axis=-1)
    n_loc = 3 * BLOCK
    o = (jnp.einsum('bnkgqs,bnskd->bnqkgd', p[..., :n_loc], vb)
         + jnp.einsum('bnkgqc,bckd->bnqkgd', p[..., n_loc:n_loc + C], vc))
    return o.reshape(bsz, L, N_HEADS * HEAD_DIM)


def _ctx_attention(qc, kc, vc, sink):
    bsz, C = qc.shape[0], qc.shape[1]
    grp = N_HEADS // N_KV_HEADS
    qg = qc.reshape(bsz, C, N_KV_HEADS, grp, HEAD_DIM)
    s = jnp.einsum('bqkgd,bskd->bkgqs', qg, kc, preferred_element_type=F32) * HEAD_DIM ** -0.5
    s_sink = jnp.broadcast_to(sink.reshape(N_KV_HEADS, grp)[None, :, :, None, None], s.shape[:-1] + (1,))
    p = jax.nn.softmax(jnp.concatenate([s, s_sink], axis=-1), axis=-1)[..., :C]
    o = jnp.einsum('bkgqs,bskd->bqkgd', p, vc)
    return o.reshape(bsz, C, N_HEADS * HEAD_DIM)


def _ssm_discretize(lam_re, lam_im, log_dt, b_re, b_im):
    lr = jnp.minimum(lam_re, LAMBDA_RE_MAX)
    li = lam_im
    dt = jnp.exp(log_dt)[..., None]
    mag = jnp.exp(lr * dt)
    a_re = mag * jnp.cos(li * dt)
    a_im = mag * jnp.sin(li * dt)
    den = lr * lr + li * li
    nr = a_re - 1.0
    coef_re = ((nr * lr + a_im * li) / den)[..., None]
    coef_im = ((a_im * lr - nr * li) / den)[..., None]
    return a_re, a_im, coef_re * b_re - coef_im * b_im, coef_re * b_im + coef_im * b_re


def _complex_powers(a_re, a_im, n):
    pr, pi = [jnp.ones_like(a_re)], [jnp.zeros_like(a_im)]
    for _ in range(n):
        pr.append(pr[-1] * a_re - pi[-1] * a_im)
        pi.append(pr[-2] * a_im + pi[-1] * a_re)
    return jnp.stack(pr), jnp.stack(pi)


def _ssm_operators(lam_re, lam_im, log_dt, b_re, b_im, c_re, c_im):
    T, G, P, H = SSM_T, SSM_G, SSM_P, SSM_H
    hp = lax.Precision.HIGHEST
    a_re, a_im, bb_re, bb_im = _ssm_discretize(lam_re, lam_im, log_dt, b_re, b_im)
    pr, pi = _complex_powers(a_re, a_im, T)
    ca_re = c_re[None] * pr[:T, :, :, None, :] - c_im[None] * pi[:T, :, :, None, :]
    ca_im = c_re[None] * pi[:T, :, :, None, :] + c_im[None] * pr[:T, :, :, None, :]
    kern = (jnp.einsum('tdgkp,dgph->tdgkh', ca_re, bb_re, precision=hp)
            - jnp.einsum('tdgkp,dgph->tdgkh', ca_im, bb_im, precision=hp))
    s_idx = jnp.arange(T)[:, None]
    t_idx = jnp.arange(T)[None, :]
    lag_f = t_idx - s_idx
    lag_b = s_idx - t_idx
    toep_f = jnp.where((lag_f >= 0)[:, :, None, None, None], kern[jnp.clip(lag_f, 0, T - 1), 0], 0.0)
    toep_b = jnp.where((lag_b >= 0)[:, :, None, None, None], kern[jnp.clip(lag_b, 0, T - 1), 1], 0.0)
    w_toep = (toep_f + toep_b).transpose(2, 0, 4, 1, 3).reshape(G, T * H, T * H)

    pw_re = jnp.stack([pr[:T, 0][::-1], pr[:T, 1]], axis=1)
    pw_im = jnp.stack([pi[:T, 0][::-1], pi[:T, 1]], axis=1)
    st_re = pw_re[..., None] * bb_re[None] - pw_im[..., None] * bb_im[None]
    st_im = pw_re[..., None] * bb_im[None] + pw_im[..., None] * bb_re[None]
    w_state = jnp.concatenate([st_re[:, 0], st_re[:, 1], st_im[:, 0], st_im[:, 1]], axis=2)
    w_state = w_state.transpose(1, 0, 3, 2).reshape(G, T * H, 4 * P)

    qw_re = jnp.stack([pr[1:, 0], pr[1:, 1][::-1]], axis=1)
    qw_im = jnp.stack([pi[1:, 0], pi[1:, 1][::-1]], axis=1)
    en_re = c_re[None] * qw_re[:, :, :, None, :] - c_im[None] * qw_im[:, :, :, None, :]
    en_im = -(c_re[None] * qw_im[:, :, :, None, :] + c_im[None] * qw_re[:, :, :, None, :])
    w_enter = jnp.concatenate([en_re[:, 0], en_re[:, 1], en_im[:, 0], en_im[:, 1]], axis=3)
    w_enter = w_enter.transpose(1, 3, 0, 2).reshape(G, 4 * P, T * H)

    qr, qi = _complex_powers(pr[T], pi[T], 8)
    r_idx = jnp.arange(8)[None, :]
    j_idx = jnp.arange(8)[:, None]

    def lanes(fwd_pow, bwd_pow, fwd_mask, bwd_mask):
        out = []
        for q in (qr, qi):
            f = jnp.where(fwd_mask[..., None, None], q[jnp.clip(fwd_pow, 0, 8), 0], 0.0)
            b = jnp.where(bwd_mask[..., None, None], q[jnp.clip(bwd_pow, 0, 8), 1], 0.0)
            out.append(jnp.concatenate([f, b], axis=-1))
        return out

    coef = lanes(r_idx - 1 - j_idx, j_idx - 1 - r_idx, r_idx > j_idx, r_idx < j_idx)
    rr = jnp.arange(8)
    ones = jnp.ones((8,), bool)
    powr = lanes(rr, 7 - rr, ones, ones)
    powt = lanes(7 - rr, rr, ones, ones)
    a8 = lanes(jnp.full((8,), 8), jnp.full((8,), 8), ones, ones)
    consts = jnp.concatenate(
        [jnp.concatenate([coef[i], powr[i][None], powt[i][None], a8[i][None]], axis=0) for i in range(2)],
        axis=0)
    consts = consts.transpose(2, 0, 1, 3)
    return w_state.astype(BF16), w_toep.astype(BF16), w_enter.astype(BF16), consts


def _ssm_kernel(x_ref, ws_ref, wt_ref, we_ref, k_ref, d_ref, y_ref, s_ref, e_ref, l_ref, c_ref,
                *, n_batch, lat_tiles, ctx_tiles):
    P2 = 2 * SSM_P
    n_tiles = n_batch * (lat_tiles + ctx_tiles)
    x = x_ref[0]
    xb = x.astype(BF16)
    s_ref[...] = jnp.dot(xb, ws_ref[0], preferred_element_type=F32)

    ptr, pti = k_ref[0, 9], k_ref[0, SCAN_K + 9]
    for t in range(n_tiles):
        rows = slice(8 * t, 8 * t + 8)
        a, b = s_ref[rows, :P2], s_ref[rows, P2:]
        er = jnp.zeros((8, P2), F32)
        ei = jnp.zeros((8, P2), F32)
        for j in range(8):
            cr, ci = k_ref[0, j], k_ref[0, SCAN_K + j]
            ar = jnp.broadcast_to(a[j:j + 1, :], (8, P2))
            ai = jnp.broadcast_to(b[j:j + 1, :], (8, P2))
            er = er + cr * ar - ci * ai
            ei = ei + cr * ai + ci * ar
        e_ref[rows, :P2] = er
        e_ref[rows, P2:] = ei
        l_ref[t:t + 1, :P2] = jnp.sum(ptr * a - pti * b, axis=0, keepdims=True)
        l_ref[t:t + 1, P2:] = jnp.sum(ptr * b + pti * a, axis=0, keepdims=True)

    a8r, a8i = k_ref[0, 10, 0:1, :], k_ref[0, SCAN_K + 10, 0:1, :]
    for bi in range(n_batch):
        lat = [bi * lat_tiles + i for i in range(lat_tiles)]
        ctx = [n_batch * lat_tiles + bi * ctx_tiles + i for i in range(ctx_tiles)]
        for order, lo in ((ctx + lat, 0), (ctx[::-1] + lat[::-1], SSM_P)):
            cr = jnp.zeros((1, P2), F32)
            ci = jnp.zeros((1, P2), F32)
            for t in order:
                c_ref[t:t + 1, lo:lo + SSM_P] = cr[:, lo:lo + SSM_P]
                c_ref[t:t + 1, P2 + lo:P2 + lo + SSM_P] = ci[:, lo:lo + SSM_P]
                lr, li = l_ref[t:t + 1, :P2], l_ref[t:t + 1, P2:]
                cr, ci = a8r * cr - a8i * ci + lr, a8r * ci + a8i * cr + li

    pwr, pwi = k_ref[0, 8], k_ref[0, SCAN_K + 8]
    for t in range(n_tiles):
        rows = slice(8 * t, 8 * t + 8)
        cr = jnp.broadcast_to(c_ref[t:t + 1, :P2], (8, P2))
        ci = jnp.broadcast_to(c_ref[t:t + 1, P2:], (8, P2))
        e_ref[rows, :P2] = e_ref[rows, :P2] + pwr * cr - pwi * ci
        e_ref[rows, P2:] = e_ref[rows, P2:] + pwr * ci + pwi * cr

    y = jnp.dot(xb, wt_ref[0], preferred_element_type=F32)
    y = y + jnp.dot(e_ref[...].astype(BF16), we_ref[0], preferred_element_type=F32)
    y_ref[0] = y + x * d_ref[0]


def _ssm_scan(u, uc, w_state, w_toep, w_enter, consts, d_skip):
    bsz, L, _ = u.shape
    C = uc.shape[1]
    T, G, H = SSM_T, SSM_G, SSM_H
    nl, nc = L // T, C // T
    assert nl % 8 == 0 and nc % 8 == 0

    def to_chunks(v, n):
        return v.reshape(bsz, n, T, G, H).transpose(3, 0, 1, 2, 4).reshape(G, bsz * n, T * H)

    def from_chunks(v, n):
        return v.reshape(G, bsz, n, T, H).transpose(1, 2, 3, 0, 4).reshape(bsz, n * T, G * H)

    x = jnp.concatenate([to_chunks(u, nl), to_chunks(uc, nc)], axis=1)
    R = bsz * (nl + nc)
    n_tiles = R // 8
    d_tile = jnp.tile(d_skip.reshape(G, 1, H), (1, T, 1)).reshape(G, 1, T * H)
    gspec = lambda shape: pl.BlockSpec((1,) + shape, lambda g: (g,) + (0,) * len(shape))
    y = pl.pallas_call(
        functools.partial(_ssm_kernel, n_batch=bsz, lat_tiles=nl // 8, ctx_tiles=nc // 8),
        grid=(G,),
        in_specs=[gspec((R, T * H)), gspec((T * H, 4 * SSM_P)), gspec((T * H, T * H)),
                  gspec((4 * SSM_P, T * H)), gspec((2 * SCAN_K, 8, 2 * SSM_P)), gspec((1, T * H))],
        out_specs=gspec((R, T * H)),
        out_shape=jax.ShapeDtypeStruct((G, R, T * H), F32),
        scratch_shapes=[pltpu.VMEM((R, 4 * SSM_P), F32), pltpu.VMEM((R, 4 * SSM_P), F32),
                        pltpu.VMEM((n_tiles, 4 * SSM_P), F32), pltpu.VMEM((n_tiles, 4 * SSM_P), F32)],
        compiler_params=pltpu.CompilerParams(
            dimension_semantics=("arbitrary",), vmem_limit_bytes=VMEM_LIMIT_BYTES),
        name="ssm_scan",
    )(x, w_state, w_toep, w_enter, consts, d_tile)
    return from_chunks(y[:, :bsz * nl], nl), from_chunks(y[:, bsz * nl:], nc)


def _ssm_glu(y, w, b):
    y = jax.nn.gelu(y)
    return y * jax.nn.sigmoid(y @ w + b)


def _ssm_branch(u, uc, need_ctx, lam_re, lam_im, log_dt, b_re, b_im, c_re, c_im, d_skip, glu_w, glu_b):
    ops = _ssm_operators(lam_re, lam_im, log_dt, b_re, b_im, c_re, c_im)
    y_l, y_c = _ssm_scan(u, uc, *ops, d_skip)
    out_l = _ssm_glu(y_l, glu_w, glu_b)
    out_c = _ssm_glu(y_c, glu_w, glu_b) if need_ctx else None
    return out_l, out_c


def _conv_branch(p, conv_w, conv_b, ln_g, ln_b):
    val, gt = jnp.split(p, 2, axis=-1)
    h = val * jax.nn.sigmoid(gt)
    h = lax.conv_general_dilated(
        h, conv_w[:, None, :], window_strides=(1,),
        padding=[(CONV_K // 2, CONV_K // 2)], dimension_numbers=('NWC', 'WIO', 'NWC'),
        feature_group_count=GROUP_W) + conv_b
    return jax.nn.silu(_layer_norm(h, ln_g, ln_b))


def _fourier_branch(p):
    bsz, L = p.shape[0], p.shape[1]
    f = p.reshape(bsz, L, FFT_HEADS, GROUP_W // FFT_HEADS)
    f = jnp.fft.fft2(f, axes=(1, 3), norm='ortho').real
    return f.reshape(bsz, L, GROUP_W)


def _mixer(h, hc, row, col, need_ctx, w_in, w_out, sink, lam_re, lam_im, log_dt, b_re, b_im,
           c_re, c_im, d_skip, glu_w, glu_b, conv_w, conv_b, ln_g, ln_b):
    bsz, L = h.shape[0], h.shape[1]
    C = hc.shape[1]
    p = h @ w_in
    pc = hc @ (w_in if need_ctx else w_in[:, :CTX_COLS])
    q = _axial_rope(p[..., Q_OFF:CONV_OFF].reshape(bsz, L, N_HEADS, HEAD_DIM), row, col)
    k = _axial_rope(p[..., K_OFF:V_OFF].reshape(bsz, L, N_KV_HEADS, HEAD_DIM), row, col)
    v = p[..., V_OFF:SSM_OFF].reshape(bsz, L, N_KV_HEADS, HEAD_DIM)
    kc = pc[..., K_OFF:V_OFF].reshape(bsz, C, N_KV_HEADS, HEAD_DIM)
    vc = pc[..., V_OFF:SSM_OFF].reshape(bsz, C, N_KV_HEADS, HEAD_DIM)
    att = _window_attention(q, k, v, kc, vc, sink)
    ssm_l, ssm_c = _ssm_branch(p[..., SSM_OFF:Q_OFF], pc[..., SSM_OFF:Q_OFF], need_ctx, lam_re, lam_im,
                               log_dt, b_re, b_im, c_re, c_im, d_skip, glu_w, glu_b)
    conv_l = _conv_branch(p[..., CONV_OFF:FFT_OFF], conv_w, conv_b, ln_g, ln_b)
    fft_l = _fourier_branch(p[..., FFT_OFF:D_IN])
    out = jnp.concatenate([att, ssm_l, conv_l, fft_l], axis=-1) @ w_out
    if not need_ctx:
        return out, None
    qc = pc[..., Q_OFF:CONV_OFF].reshape(bsz, C, N_HEADS, HEAD_DIM)
    att_c = _ctx_attention(qc, kc, vc, sink)
    conv_c = _conv_branch(pc[..., CONV_OFF:FFT_OFF], conv_w, conv_b, ln_g, ln_b)
    fft_c = _fourier_branch(pc[..., FFT_OFF:D_IN])
    out_c = jnp.concatenate([att_c, ssm_c, conv_c, fft_c], axis=-1) @ w_out
    return out, out_c


def kernel(x, c, ctx, c_ctx, w_mod, b_mod, norm_g, ffn_wi, ffn_wo, w_in, w_out, attn_sink,
           ssm_lam_re, ssm_lam_im, ssm_log_dt, ssm_b_re, ssm_b_im, ssm_c_re, ssm_c_im, ssm_d,
           ssm_glu_w, ssm_glu_b, conv_w, conv_b, conv_ln_g, conv_ln_b):
    bsz, L, d = x.shape
    C = ctx.shape[1]
    n_lat = bsz * L
    n_all = n_lat + bsz * C
    rows = L // GRID_W
    row = jnp.repeat(jnp.arange(rows), GRID_W)
    col = jnp.tile(jnp.arange(GRID_W), rows)

    cond = jnp.concatenate([c, c_ctx[None, :], jnp.zeros((MOD_ROWS - bsz - 1, d), F32)], axis=0)
    mod_all = _modulation(cond, w_mod, b_mod)
    mod_all = mod_all[:, :bsz + 1].reshape(DEPTH, bsz + 1, 3, 3, d)

    s = jnp.concatenate([x.reshape(n_lat, d), ctx.reshape(bsz * C, d)], axis=0)
    for l in range(DEPTH):
        need_ctx = l < DEPTH - 1
        g = norm_g[l]
        m = mod_all[l]
        s = _ffn_sublayer(s, m[:, 0:1], g[0], g[1], ffn_wi[l, 0], ffn_wo[l, 0], L, n_all)
        xl = s[:n_lat].reshape(bsz, L, d)
        xc = s[n_lat:].reshape(bsz, C, d)
        ml = m[:bsz, 1][:, :, None, :]
        mc = m[bsz, 1]
        h = _rms_norm(xl, g[2]) * (1 + ml[:, 1]) + ml[:, 0]
        hc = _rms_norm(xc, g[2]) * (1 + mc[1]) + mc[0]
        out, out_c = _mixer(h, hc, row, col, need_ctx, w_in[l], w_out[l], attn_sink[l],
                            ssm_lam_re[l], ssm_lam_im[l], ssm_log_dt[l], ssm_b_re[l], ssm_b_im[l],
                            ssm_c_re[l], ssm_c_im[l], ssm_d[l], ssm_glu_w[l], ssm_glu_b[l],
                            conv_w[l], conv_b[l], conv_ln_g[l], conv_ln_b[l])
        xl = xl + ml[:, 2] * _rms_norm(out, g[3])
        if need_ctx:
            xc = xc + mc[2] * _rms_norm(out_c, g[3])
            s = jnp.concatenate([xl.reshape(n_lat, d), xc.reshape(bsz * C, d)], axis=0)
            s = _ffn_sublayer(s, m[:, 2:3], g[4], g[5], ffn_wi[l, 1], ffn_wo[l, 1], L, n_all)
        else:
            s = _ffn_sublayer(xl.reshape(n_lat, d), m[:, 2:3], g[4], g[5], ffn_wi[l, 1], ffn_wo[l, 1], L, n_lat)
    return s.reshape(bsz, L, d)
```

```python
import functools
import math

import jax
import jax.numpy as jnp
from jax import lax
from jax.experimental import pallas as pl
from jax.experimental.pallas import tpu as pltpu

F32 = jnp.float32
BF16 = jnp.bfloat16

D_MODEL = 2048
DEPTH = 4
GRID_W = 64
N_MOD = 9
FFN_RES = 0.5
D_FF = 5632
RMS_EPS = 1e-6
LN_EPS = 1e-5
GROUP_W = 512
N_HEADS = 8
N_KV_HEADS = 2
HEAD_DIM = 64
WINDOW = 128
BLOCK = 128
ROPE_BASE = 10000.0
NEG_INF = -1e30
SSM_H = 16
SSM_G = 32
SSM_P = 64
LAMBDA_RE_MAX = -1e-4
SSM_T = 16
SCAN_K = 11
CONV_K = 31
CONV_HALO = 16
CONV_RB = 32
FFT_HEADS = 4
FFT_W = GROUP_W // FFT_HEADS
KV_W = N_KV_HEADS * HEAD_DIM
K_OFF = 0
V_OFF = K_OFF + KV_W
SSM_OFF = V_OFF + KV_W
Q_OFF = SSM_OFF + GROUP_W
CONV_OFF = Q_OFF + GROUP_W
FFT_OFF = CONV_OFF + 2 * GROUP_W
D_IN = FFT_OFF + GROUP_W

X_Q, X_QR, X_K, X_KR, X_V, X_U, X_CV, X_CG, X_F, X_END = 0, 512, 1024, 1280, 1536, 1792, 2304, 2816, 3328, 3840

VMEM_LIMIT_BYTES = 56 * 1024 * 1024
LANES = 128
MOD_ROWS = 8
MOD_TN = 1024
FFN_TM = 1024
FFN_TF = 256
TOK_TM = 512
DFT_TILE = 2048


FFN_VMEM_LIMIT_BYTES = 60000 * 1024


def _params(*sem, vmem=VMEM_LIMIT_BYTES):
    return pltpu.CompilerParams(dimension_semantics=sem, vmem_limit_bytes=vmem)


def _mod_kernel(c_ref, w_ref, b_ref, o_ref):
    c = c_ref[...]
    sc = (c * jax.nn.sigmoid(c)).astype(BF16)
    o_ref[0] = jnp.dot(sc, w_ref[0].astype(BF16), preferred_element_type=F32) + b_ref[0]


def _modulation(cond, w_mod, b_mod):
    depth, d, n = w_mod.shape
    return pl.pallas_call(
        _mod_kernel,
        grid=(depth, n // MOD_TN),
        in_specs=[
            pl.BlockSpec((MOD_ROWS, d), lambda l, j: (0, 0)),
            pl.BlockSpec((1, d, MOD_TN), lambda l, j: (l, 0, j)),
            pl.BlockSpec((1, 1, MOD_TN), lambda l, j: (l, 0, j)),
        ],
        out_specs=pl.BlockSpec((1, MOD_ROWS, MOD_TN), lambda l, j: (l, 0, j)),
        out_shape=jax.ShapeDtypeStruct((depth, MOD_ROWS, n), F32),
        compiler_params=_params("arbitrary", "arbitrary"),
        name="modulation",
    )(cond, w_mod, b_mod.reshape(depth, 1, n))


def _ffn_kernel(s_ref, mod_ref, gpre_ref, gpost_ref, wg_ref, wu_ref, wo_ref, o_ref, h_ref):
    j = pl.program_id(1)

    @pl.when(j == 0)
    def _():
        s = s_ref[...]
        y = s * lax.rsqrt(jnp.mean(s * s, axis=-1, keepdims=True) + RMS_EPS) * gpre_ref[...]
        h = y * (1.0 + mod_ref[0, 0, 1:2, :]) + mod_ref[0, 0, 0:1, :]
        h_ref[...] = h.astype(BF16)
        o_ref[...] = jnp.zeros_like(o_ref)

    h = h_ref[...]
    gt = jnp.dot(h, wg_ref[...].astype(BF16), preferred_element_type=F32)
    up = jnp.dot(h, wu_ref[...].astype(BF16), preferred_element_type=F32)
    a = (gt * jax.nn.sigmoid(gt) * up).astype(BF16)
    o_ref[...] += jnp.dot(a, wo_ref[...].astype(BF16), preferred_element_type=F32)

    @pl.when(j == pl.num_programs(1) - 1)
    def _():
        acc = o_ref[...]
        y = acc * lax.rsqrt(jnp.mean(acc * acc, axis=-1, keepdims=True) + RMS_EPS) * gpost_ref[...]
        o_ref[...] = s_ref[...] + (FFN_RES * mod_ref[0, 0, 2:3, :]) * y


def _ffn_sublayer(s, mod, g_pre, g_post, wi, wo, rows_per_mod, n_rows):
    d = s.shape[1]
    n_ff = wo.shape[0]
    tiles_per_mod = rows_per_mod // FFN_TM
    nj = n_ff // FFN_TF
    last_mod = mod.shape[0] - 1
    return pl.pallas_call(
        _ffn_kernel,
        grid=(n_rows // FFN_TM, nj),
        in_specs=[
            pl.BlockSpec((FFN_TM, d), lambda i, j: (i, 0)),
            pl.BlockSpec((1, 1, 3, d), lambda i, j: (jnp.minimum(i // tiles_per_mod, last_mod), 0, 0, 0)),
            pl.BlockSpec((1, d), lambda i, j: (0, 0)),
            pl.BlockSpec((1, d), lambda i, j: (0, 0)),
            pl.BlockSpec((d, FFN_TF), lambda i, j: (0, j)),
            pl.BlockSpec((d, FFN_TF), lambda i, j: (0, j + nj)),
            pl.BlockSpec((FFN_TF, d), lambda i, j: (j, 0)),
        ],
        out_specs=pl.BlockSpec((FFN_TM, d), lambda i, j: (i, 0)),
        out_shape=jax.ShapeDtypeStruct((n_rows, d), F32),
        scratch_shapes=[pltpu.VMEM((FFN_TM, d), BF16)],
        compiler_params=_params("arbitrary", "arbitrary", vmem=FFN_VMEM_LIMIT_BYTES),
        name="ffn_sublayer",
    )(s, mod, g_pre.reshape(1, d), g_post.reshape(1, d), wi, wi, wo)


def _rot_perm():
    q = HEAD_DIM // 4
    perm = jnp.concatenate([jnp.arange(q, 2 * q), jnp.arange(0, q), jnp.arange(3 * q, 4 * q), jnp.arange(2 * q, 3 * q)])
    sign = jnp.concatenate([-jnp.ones(q), jnp.ones(q), -jnp.ones(q), jnp.ones(q)]).astype(F32)
    return perm, sign


def _extended_w_in(w_in):
    lead = w_in.shape[:-1]
    perm, sign = _rot_perm()
    scale = HEAD_DIM ** -0.5

    def heads(w, n):
        return w.reshape(lead + (n, HEAD_DIM))

    def rot(wh):
        return wh[..., perm] * sign

    def dup(wh):
        return jnp.concatenate([wh[..., None, :], wh[..., None, :]], axis=-2).reshape(lead + (2 * KV_W,))

    wq = heads(w_in[..., Q_OFF:CONV_OFF], N_HEADS) * scale
    wk = heads(w_in[..., K_OFF:V_OFF], N_KV_HEADS)
    wv = heads(w_in[..., V_OFF:SSM_OFF], N_KV_HEADS)
    w = jnp.concatenate([wq.reshape(lead + (GROUP_W,)), rot(wq).reshape(lead + (GROUP_W,)), dup(wk), dup(rot(wk)),
                         dup(wv), w_in[..., SSM_OFF:Q_OFF], w_in[..., CONV_OFF:FFT_OFF], w_in[..., FFT_OFF:D_IN]],
                        axis=-1)
    return w.astype(BF16)


def _rope_tables(L, n_tail):
    pos = jnp.arange(L)
    half = HEAD_DIM // 4
    inv = ROPE_BASE ** (-jnp.arange(half, dtype=F32) / half)
    ang_r = (pos // GRID_W).astype(F32)[:, None] * inv[None, :]
    ang_c = (pos % GRID_W).astype(F32)[:, None] * inv[None, :]
    ang = jnp.concatenate([ang_r, ang_r, ang_c, ang_c], axis=1)
    cos = jnp.concatenate([jnp.tile(jnp.cos(ang), (1, 4)), jnp.ones((n_tail, 4 * HEAD_DIM), F32)], axis=0)
    sin = jnp.concatenate([jnp.tile(jnp.sin(ang), (1, 4)), jnp.zeros((n_tail, 4 * HEAD_DIM), F32)], axis=0)
    return cos, sin


def _split_bf16(x):
    hi = x.astype(BF16)
    return hi, (x - hi.astype(F32)).astype(BF16)


def _channel_dft():
    c = jnp.arange(FFT_W)
    ang = (2.0 * math.pi / FFT_W) * ((c[:, None] * c[None, :]) % FFT_W).astype(F32)
    hi, lo = _split_bf16(jnp.concatenate([jnp.cos(ang), jnp.sin(ang)], axis=1))
    return jnp.stack([hi, lo])


def _position_dft(n):
    k = jnp.arange(n)
    ang = (2.0 * math.pi / n) * ((k[:, None] * k[None, :]) % n).astype(F32)
    return jnp.concatenate([jnp.cos(ang), -jnp.sin(ang)], axis=1).astype(BF16)


def _inproj_kernel(s_ref, mod_ref, g_ref, cos_ref, sin_ref, w_ref, cd_ref,
                   q_ref, kd_ref, vd_ref, u_ref, cg_ref, f_ref):
    s = s_ref[...]
    y = s * lax.rsqrt(jnp.mean(s * s, axis=-1, keepdims=True) + RMS_EPS) * g_ref[...]
    h = (y * (1.0 + mod_ref[0, 0, 1:2, :]) + mod_ref[0, 0, 0:1, :]).astype(BF16)

    def proj(lo, hi):
        return jnp.dot(h, w_ref[:, lo:hi], preferred_element_type=F32)

    cos, sin = cos_ref[...], sin_ref[...]
    w4 = 4 * HEAD_DIM
    for j in range(GROUP_W // w4):
        q = proj(X_Q + j * w4, X_Q + (j + 1) * w4)
        qr = proj(X_QR + j * w4, X_QR + (j + 1) * w4)
        q_ref[:, j * w4:(j + 1) * w4] = (q * cos + qr * sin).astype(BF16)
    kd_ref[...] = (proj(X_K, X_KR) * cos + proj(X_KR, X_V) * sin).astype(BF16)
    vd_ref[...] = proj(X_V, X_U).astype(BF16)
    u_ref[...] = proj(X_U, X_CV)
    val = proj(X_CV, X_CG)
    cg_ref[...] = val * jax.nn.sigmoid(proj(X_CG, X_F))
    f_hi, f_lo = _split_bf16(proj(X_F, X_END))
    c_hi, c_lo = cd_ref[0], cd_ref[1]
    for hd in range(FFT_HEADS):
        sl = slice(hd * FFT_W, (hd + 1) * FFT_W)
        r = (jnp.dot(f_hi[:, sl], c_hi, preferred_element_type=F32)
             + jnp.dot(f_lo[:, sl], c_hi, preferred_element_type=F32)
             + jnp.dot(f_hi[:, sl], c_lo, preferred_element_type=F32))
        f_ref[0, :, sl] = r[:, :FFT_W].astype(BF16)
        f_ref[1, :, sl] = r[:, FFT_W:].astype(BF16)


def _in_projection(s, mod, g, cos, sin, w_ext, cdft, L, n_lat, bsz):
    n_rows, d = s.shape
    tm = TOK_TM
    lat_tiles, seq_tiles = n_lat // tm, L // tm
    row = lambda i: (i, 0)
    tab = lambda i: (jnp.where(i < lat_tiles, i % seq_tiles, seq_tiles + i - lat_tiles), 0)
    const2 = lambda i: (0, 0)
    outs = [((n_rows, GROUP_W), BF16), ((n_rows, 2 * KV_W), BF16), ((n_rows, 2 * KV_W), BF16),
            ((n_rows, GROUP_W), F32), ((n_rows, GROUP_W), F32)]
    return pl.pallas_call(
        _inproj_kernel,
        grid=(n_rows // tm,),
        in_specs=[
            pl.BlockSpec((tm, d), row),
            pl.BlockSpec((1, 1, 3, d), lambda i: (jnp.minimum(i // seq_tiles, bsz), 0, 0, 0)),
            pl.BlockSpec((1, d), const2),
            pl.BlockSpec((tm, 4 * HEAD_DIM), tab),
            pl.BlockSpec((tm, 4 * HEAD_DIM), tab),
            pl.BlockSpec((d, X_END), const2),
            pl.BlockSpec((2, FFT_W, 2 * FFT_W), lambda i: (0, 0, 0)),
        ],
        out_specs=[pl.BlockSpec((tm, shp[1]), row) for shp, _ in outs]
        + [pl.BlockSpec((2, tm, GROUP_W), lambda i: (0, i, 0))],
        out_shape=[jax.ShapeDtypeStruct(shp, dt) for shp, dt in outs]
        + [jax.ShapeDtypeStruct((2, n_rows, GROUP_W), BF16)],
        compiler_params=_params("arbitrary"),
        name="in_projection",
    )(s, mod, g.reshape(1, d), cos, sin, w_ext, cdft)


def _attn_kernel(sink_ref, q_ref, *refs, local):
    if local:
        kp_ref, kc_ref, kn_ref, kx_ref, vp_ref, vc_ref, vn_ref, vx_ref, o_ref = refs
    else:
        kx_ref, vx_ref, _, o_ref = refs
    tq = q_ref.shape[0]
    grp = N_HEADS // N_KV_HEADS
    lane = lax.broadcasted_iota(jnp.int32, (1, LANES), 1)
    low = (lane < HEAD_DIM)
    half_mask = (low.astype(BF16), (~low).astype(BF16))
    rows = lax.broadcasted_iota(jnp.int32, (grp * tq, 1), 0)
    if local:
        n = pl.program_id(1)
        nk = 3 * BLOCK + kx_ref.shape[0]
        r = rows & (tq - 1)
        col = lax.broadcasted_iota(jnp.int32, (1, nk), 1)
        prev_ok = (col < BLOCK) & (r <= col) & (n > 0)
        next_ok = (col >= 2 * BLOCK) & (col < 3 * BLOCK) & (col - 2 * BLOCK <= r) & (n < pl.num_programs(1) - 1)
        always = ((col >= BLOCK) & (col < 2 * BLOCK)) | (col >= 3 * BLOCK)
        valid = prev_ok | next_ok | always
    for kh in range(N_KV_HEADS):
        ks = slice(kh * LANES, (kh + 1) * LANES)
        if local:
            k = jnp.concatenate([kp_ref[:, ks], kc_ref[:, ks], kn_ref[:, ks], kx_ref[:, ks]], axis=0)
            v = jnp.concatenate([vp_ref[:, ks], vc_ref[:, ks], vn_ref[:, ks], vx_ref[:, ks]], axis=0)
        else:
            k, v = kx_ref[:, ks], vx_ref[:, ks]
        qs = []
        sink = jnp.zeros((grp * tq, 1), F32)
        for hh in range(grp):
            hd = kh * grp + hh
            qs.append(q_ref[:, (hd // 2) * LANES:(hd // 2 + 1) * LANES] * half_mask[hd % 2])
            sink = jnp.where((rows >= hh * tq) & (rows < (hh + 1) * tq), sink_ref[hd], sink)
        q4 = jnp.concatenate(qs, axis=0)
        sc = lax.dot_general(q4, k, (((1,), (1,)), ((), ())), preferred_element_type=F32)
        if local:
            sc = jnp.where(valid, sc, NEG_INF)
        m = jnp.maximum(jnp.max(sc, axis=-1, keepdims=True), sink)
        p = jnp.exp(sc - m)
        denom = jnp.sum(p, axis=-1, keepdims=True) + jnp.exp(sink - m)
        o = jnp.dot(p.astype(BF16), v, preferred_element_type=F32) / denom
        for j in range(grp // 2):
            blk = jnp.where(low, o[(2 * j) * tq:(2 * j + 1) * tq], o[(2 * j + 1) * tq:(2 * j + 2) * tq])
            jb = kh * (grp // 2) + j
            o_ref[:, jb * LANES:(jb + 1) * LANES] = blk.astype(BF16)


def _attention(q, kd, vd, sink, L, C, n_lat, bsz, need_ctx):
    n_rows = q.shape[0]
    nb = L // BLOCK
    kw = 2 * KV_W
    smem = pl.BlockSpec(memory_space=pltpu.SMEM)
    blk = lambda f: pl.BlockSpec((BLOCK, kw), f)
    prev = lambda b, n: (b * nb + jnp.maximum(n - 1, 0), 0)
    cur = lambda b, n: (b * nb + n, 0)
    nxt = lambda b, n: (b * nb + jnp.minimum(n + 1, nb - 1), 0)
    cx = lambda b, n: (n_lat // C + b, 0)
    att = pl.pallas_call(
        functools.partial(_attn_kernel, local=True),
        grid=(bsz, nb),
        in_specs=[smem, pl.BlockSpec((BLOCK, GROUP_W), cur),
                  blk(prev), blk(cur), blk(nxt), pl.BlockSpec((C, kw), cx),
                  blk(prev), blk(cur), blk(nxt), pl.BlockSpec((C, kw), cx)],
        out_specs=pl.BlockSpec((BLOCK, GROUP_W), cur),
        out_shape=jax.ShapeDtypeStruct((n_rows, GROUP_W), BF16),
        compiler_params=_params("arbitrary", "arbitrary"),
        name="window_attention",
    )(sink, q, kd, kd, kd, kd, vd, vd, vd, vd)
    if not need_ctx:
        return att
    cb = lambda b: (n_lat // C + b, 0)
    return pl.pallas_call(
        functools.partial(_attn_kernel, local=False),
        grid=(bsz,),
        in_specs=[smem, pl.BlockSpec((C, GROUP_W), cb), pl.BlockSpec((C, kw), cb), pl.BlockSpec((C, kw), cb),
                  pl.BlockSpec(memory_space=pl.ANY)],
        out_specs=pl.BlockSpec((C, GROUP_W), cb),
        out_shape=jax.ShapeDtypeStruct((n_rows, GROUP_W), BF16),
        input_output_aliases={4: 0},
        compiler_params=_params("arbitrary"),
        name="context_attention",
    )(sink, q, kd, vd, att)


def _conv_kernel(*refs, tiled):
    if tiled:
        xp_ref, xc_ref, xn_ref, w_ref, b_ref, g_ref, beta_ref, o_ref, ext_ref = refs
    else:
        xc_ref, w_ref, b_ref, g_ref, beta_ref, _, o_ref, ext_ref = refs
    tt = xc_ref.shape[0]
    H = CONV_HALO
    zeros = jnp.zeros((H, GROUP_W), F32)
    if tiled:
        n = pl.program_id(1)
        ext_ref[0:H, :] = jnp.where(n > 0, xp_ref[tt - H:tt, :], zeros)
        ext_ref[H + tt:2 * H + tt, :] = jnp.where(n < pl.num_programs(1) - 1, xn_ref[0:H, :], zeros)
    else:
        ext_ref[0:H, :] = zeros
        ext_ref[H + tt:2 * H + tt, :] = zeros
    ext_ref[H:H + tt, :] = xc_ref[...]
    off = H - CONV_K // 2
    for rb in range(tt // CONV_RB):
        acc = jnp.zeros((CONV_RB, GROUP_W), F32)
        for k in range(CONV_K):
            acc = acc + w_ref[k:k + 1, :] * ext_ref[pl.ds(rb * CONV_RB + k + off, CONV_RB), :]
        acc = acc + b_ref[...]
        mu = jnp.mean(acc, axis=-1, keepdims=True)
        cen = acc - mu
        var = jnp.mean(cen * cen, axis=-1, keepdims=True)
        y = cen * lax.rsqrt(var + LN_EPS) * g_ref[...] + beta_ref[...]
        o_ref[rb * CONV_RB:(rb + 1) * CONV_RB, :] = (y * jax.nn.sigmoid(y)).astype(BF16)


def _conv_group(cg, conv_w, conv_b, ln_g, ln_b, L, C, n_lat, bsz, need_ctx):
    n_rows = cg.shape[0]
    tt = TOK_TM
    nt = L // tt
    vec = lambda a: a.reshape(1, GROUP_W)
    cvec = pl.BlockSpec((1, GROUP_W), lambda *_: (0, 0))
    wspec = pl.BlockSpec((CONV_K, GROUP_W), lambda *_: (0, 0))
    blk = lambda f: pl.BlockSpec((tt, GROUP_W), f)
    cur = lambda b, n: (b * nt + n, 0)
    out = pl.pallas_call(
        functools.partial(_conv_kernel, tiled=True),
        grid=(bsz, nt),
        in_specs=[blk(lambda b, n: (b * nt + jnp.maximum(n - 1, 0), 0)), blk(cur),
                  blk(lambda b, n: (b * nt + jnp.minimum(n + 1, nt - 1), 0)), wspec, cvec, cvec, cvec],
        out_specs=blk(cur),
        out_shape=jax.ShapeDtypeStruct((n_rows, GROUP_W), BF16),
        scratch_shapes=[pltpu.VMEM((tt + 2 * CONV_HALO, GROUP_W), F32)],
        compiler_params=_params("arbitrary", "arbitrary"),
        name="conv_group",
    )(cg, cg, cg, conv_w, vec(conv_b), vec(ln_g), vec(ln_b))
    if not need_ctx:
        return out
    cb = lambda b: (n_lat // C + b, 0)
    return pl.pallas_call(
        functools.partial(_conv_kernel, tiled=False),
        grid=(bsz,),
        in_specs=[pl.BlockSpec((C, GROUP_W), cb), wspec, cvec, cvec, cvec, pl.BlockSpec(memory_space=pl.ANY)],
        out_specs=pl.BlockSpec((C, GROUP_W), cb),
        out_shape=jax.ShapeDtypeStruct((n_rows, GROUP_W), BF16),
        scratch_shapes=[pltpu.VMEM((C + 2 * CONV_HALO, GROUP_W), F32)],
        input_output_aliases={5: 0},
        compiler_params=_params("arbitrary"),
        name="conv_group_ctx",
    )(cg, conv_w, vec(conv_b), vec(ln_g), vec(ln_b), out)


def _dft_kernel(d_ref, x_ref, *refs, scale):
    o_ref, acc_ref = refs[-2:]
    kk = pl.program_id(2)

    @pl.when(kk == 0)
    def _():
        acc_ref[...] = jnp.zeros_like(acc_ref)

    acc_ref[...] += jnp.dot(d_ref[...], x_ref[0], preferred_element_type=F32)

    @pl.when(kk == pl.num_programs(2) - 1)
    def _():
        o_ref[...] = (acc_ref[...] * scale).astype(BF16)


def _dft_call(dmat, xcs, prev, n, row0, bsz, name):
    t = min(n, DFT_TILE)
    nt = n // t
    in_specs = [pl.BlockSpec((t, t), lambda b, m, kk: (m, kk)),
                pl.BlockSpec((1, t, GROUP_W), lambda b, m, kk: (kk // nt, row0 // t + b * nt + kk % nt, 0))]
    args = [dmat, xcs]
    aliases = {}
    if prev is not None:
        in_specs.append(pl.BlockSpec(memory_space=pl.ANY))
        args.append(prev)
        aliases = {2: 0}
    return pl.pallas_call(
        functools.partial(_dft_kernel, scale=float((n * FFT_W) ** -0.5)),
        grid=(bsz, nt, 2 * nt),
        in_specs=in_specs,
        out_specs=pl.BlockSpec((t, GROUP_W), lambda b, m, kk: (row0 // t + b * nt + m, 0)),
        out_shape=jax.ShapeDtypeStruct((xcs.shape[1], GROUP_W), BF16),
        scratch_shapes=[pltpu.VMEM((t, GROUP_W), F32)],
        input_output_aliases=aliases,
        compiler_params=_params("arbitrary", "arbitrary", "arbitrary"),
        name=name,
    )(*args)


def _outproj_kernel(s_ref, mod_ref, g_ref, att_ref, y_ref, cv_ref, ff_ref, gw_ref, gb_ref, wo_ref, o_ref):
    y = jax.nn.gelu(y_ref[...], approximate=True)
    z = jnp.dot(y.astype(BF16), gw_ref[...], preferred_element_type=F32) + gb_ref[...]
    ssm = (y * jax.nn.sigmoid(z)).astype(BF16)
    out = jnp.dot(att_ref[...], wo_ref[0:GROUP_W, :], preferred_element_type=F32)
    out = out + jnp.dot(ssm, wo_ref[GROUP_W:2 * GROUP_W, :], preferred_element_type=F32)
    out = out + jnp.dot(cv_ref[...], wo_ref[2 * GROUP_W:3 * GROUP_W, :], preferred_element_type=F32)
    out = out + jnp.dot(ff_ref[...], wo_ref[3 * GROUP_W:4 * GROUP_W, :], preferred_element_type=F32)
    yn = out * lax.rsqrt(jnp.mean(out * out, axis=-1, keepdims=True) + RMS_EPS) * g_ref[...]
    o_ref[...] = s_ref[...] + mod_ref[0, 0, 2:3, :] * yn


def _out_projection(s, mod, g, att, y, cv, ff, glu_w, glu_b, w_out, L, bsz, n_rows):
    d = s.shape[1]
    tm = TOK_TM
    seq_tiles = L // tm
    row = lambda i: (i, 0)
    const2 = lambda i: (0, 0)
    grp = pl.BlockSpec((tm, GROUP_W), row)
    return pl.pallas_call(
        _outproj_kernel,
        grid=(n_rows // tm,),
        in_specs=[pl.BlockSpec((tm, d), row),
                  pl.BlockSpec((1, 1, 3, d), lambda i: (jnp.minimum(i // seq_tiles, bsz), 0, 0, 0)),
                  pl.BlockSpec((1, d), const2), grp, grp, grp, grp,
                  pl.BlockSpec((GROUP_W, GROUP_W), const2), pl.BlockSpec((1, GROUP_W), const2),
                  pl.BlockSpec((4 * GROUP_W, d), const2)],
        out_specs=pl.BlockSpec((tm, d), row),
        out_shape=jax.ShapeDtypeStruct(s.shape, F32),
        input_output_aliases={0: 0},
        compiler_params=_params("arbitrary"),
        name="out_projection",
    )(s, mod, g.reshape(1, d), att, y, cv, ff, glu_w.astype(BF16), glu_b.reshape(1, GROUP_W), w_out.astype(BF16))


def _ssm_discretize(lam_re, lam_im, log_dt, b_re, b_im):
    lr = jnp.minimum(lam_re, LAMBDA_RE_MAX)
    li = lam_im
    dt = jnp.exp(log_dt)[..., None]
    mag = jnp.exp(lr * dt)
    a_re = mag * jnp.cos(li * dt)
    a_im = mag * jnp.sin(li * dt)
    den = lr * lr + li * li
    nr = a_re - 1.0
    coef_re = ((nr * lr + a_im * li) / den)[..., None]
    coef_im = ((a_im * lr - nr * li) / den)[..., None]
    return a_re, a_im, coef_re * b_re - coef_im * b_im, coef_re * b_im + coef_im * b_re


def _complex_powers(a_re, a_im, n):
    pr, pi = [jnp.ones_like(a_re)], [jnp.zeros_like(a_im)]
    for _ in range(n):
        pr.append(pr[-1] * a_re - pi[-1] * a_im)
        pi.append(pr[-2] * a_im + pi[-1] * a_re)
    return jnp.stack(pr), jnp.stack(pi)


def _ssm_operators(lam_re, lam_im, log_dt, b_re, b_im, c_re, c_im):
    T, G, P, H = SSM_T, SSM_G, SSM_P, SSM_H
    hp = lax.Precision.HIGHEST
    a_re, a_im, bb_re, bb_im = _ssm_discretize(lam_re, lam_im, log_dt, b_re, b_im)
    pr, pi = _complex_powers(a_re, a_im, T)
    ca_re = c_re[None] * pr[:T, :, :, None, :] - c_im[None] * pi[:T, :, :, None, :]
    ca_im = c_re[None] * pi[:T, :, :, None, :] + c_im[None] * pr[:T, :, :, None, :]
    kern = (jnp.einsum('tdgkp,dgph->tdgkh', ca_re, bb_re, precision=hp)
            - jnp.einsum('tdgkp,dgph->tdgkh', ca_im, bb_im, precision=hp))
    s_idx = jnp.arange(T)[:, None]
    t_idx = jnp.arange(T)[None, :]
    lag_f = t_idx - s_idx
    lag_b = s_idx - t_idx
    toep_f = jnp.where((lag_f >= 0)[:, :, None, None, None], kern[jnp.clip(lag_f, 0, T - 1), 0], 0.0)
    toep_b = jnp.where((lag_b >= 0)[:, :, None, None, None], kern[jnp.clip(lag_b, 0, T - 1), 1], 0.0)
    w_toep = (toep_f + toep_b).transpose(2, 0, 4, 1, 3).reshape(G, T * H, T * H)

    pw_re = jnp.stack([pr[:T, 0][::-1], pr[:T, 1]], axis=1)
    pw_im = jnp.stack([pi[:T, 0][::-1], pi[:T, 1]], axis=1)
    st_re = pw_re[..., None] * bb_re[None] - pw_im[..., None] * bb_im[None]
    st_im = pw_re[..., None] * bb_im[None] + pw_im[..., None] * bb_re[None]
    w_state = jnp.concatenate([st_re[:, 0], st_re[:, 1], st_im[:, 0], st_im[:, 1]], axis=2)
    w_state = w_state.transpose(1, 0, 3, 2).reshape(G, T * H, 4 * P)

    qw_re = jnp.stack([pr[1:, 0], pr[1:, 1][::-1]], axis=1)
    qw_im = jnp.stack([pi[1:, 0], pi[1:, 1][::-1]], axis=1)
    en_re = c_re[None] * qw_re[:, :, :, None, :] - c_im[None] * qw_im[:, :, :, None, :]
    en_im = -(c_re[None] * qw_im[:, :, :, None, :] + c_im[None] * qw_re[:, :, :, None, :])
    w_enter = jnp.concatenate([en_re[:, 0], en_re[:, 1], en_im[:, 0], en_im[:, 1]], axis=3)
    w_enter = w_enter.transpose(1, 3, 0, 2).reshape(G, 4 * P, T * H)

    qr, qi = _complex_powers(pr[T], pi[T], 8)
    r_idx = jnp.arange(8)[None, :]
    j_idx = jnp.arange(8)[:, None]

    def lanes(fwd_pow, bwd_pow, fwd_mask, bwd_mask):
        out = []
        for q in (qr, qi):
            f = jnp.where(fwd_mask[..., None, None], q[jnp.clip(fwd_pow, 0, 8), 0], 0.0)
            b = jnp.where(bwd_mask[..., None, None], q[jnp.clip(bwd_pow, 0, 8), 1], 0.0)
            out.append(jnp.concatenate([f, b], axis=-1))
        return out

    coef = lanes(r_idx - 1 - j_idx, j_idx - 1 - r_idx, r_idx > j_idx, r_idx < j_idx)
    rr = jnp.arange(8)
    ones = jnp.ones((8,), bool)
    powr = lanes(rr, 7 - rr, ones, ones)
    powt = lanes(7 - rr, rr, ones, ones)
    a8 = lanes(jnp.full((8,), 8), jnp.full((8,), 8), ones, ones)
    consts = jnp.concatenate(
        [jnp.concatenate([coef[i], powr[i][None], powt[i][None], a8[i][None]], axis=0) for i in range(2)],
        axis=0)
    consts = consts.transpose(2, 0, 1, 3)
    return w_state.astype(BF16), w_toep.astype(BF16), w_enter.astype(BF16), consts


def _ssm_kernel(x_ref, ws_ref, wt_ref, we_ref, k_ref, d_ref, y_ref, s_ref, e_ref, l_ref, c_ref,
                *, n_batch, lat_tiles, ctx_tiles):
    P2 = 2 * SSM_P
    n_tiles = n_batch * (lat_tiles + ctx_tiles)
    x = x_ref[0]
    xb = x.astype(BF16)
    s_ref[...] = jnp.dot(xb, ws_ref[0], preferred_element_type=F32)

    ptr, pti = k_ref[0, 9], k_ref[0, SCAN_K + 9]
    for t in range(n_tiles):
        rows = slice(8 * t, 8 * t + 8)
        a, b = s_ref[rows, :P2], s_ref[rows, P2:]
        er = jnp.zeros((8, P2), F32)
        ei = jnp.zeros((8, P2), F32)
        for j in range(8):
            cr, ci = k_ref[0, j], k_ref[0, SCAN_K + j]
            ar = jnp.broadcast_to(a[j:j + 1, :], (8, P2))
            ai = jnp.broadcast_to(b[j:j + 1, :], (8, P2))
            er = er + cr * ar - ci * ai
            ei = ei + cr * ai + ci * ar
        e_ref[rows, :P2] = er
        e_ref[rows, P2:] = ei
        l_ref[t:t + 1, :P2] = jnp.sum(ptr * a - pti * b, axis=0, keepdims=True)
        l_ref[t:t + 1, P2:] = jnp.sum(ptr * b + pti * a, axis=0, keepdims=True)

    a8r, a8i = k_ref[0, 10, 0:1, :], k_ref[0, SCAN_K + 10, 0:1, :]
    for bi in range(n_batch):
        lat = [bi * lat_tiles + i for i in range(lat_tiles)]
        ctx = [n_batch * lat_tiles + bi * ctx_tiles + i for i in range(ctx_tiles)]
        for order, lo in ((ctx + lat, 0), (ctx[::-1] + lat[::-1], SSM_P)):
            cr = jnp.zeros((1, P2), F32)
            ci = jnp.zeros((1, P2), F32)
            for t in order:
                c_ref[t:t + 1, lo:lo + SSM_P] = cr[:, lo:lo + SSM_P]
                c_ref[t:t + 1, P2 + lo:P2 + lo + SSM_P] = ci[:, lo:lo + SSM_P]
                lr, li = l_ref[t:t + 1, :P2], l_ref[t:t + 1, P2:]
                cr, ci = a8r * cr - a8i * ci + lr, a8r * ci + a8i * cr + li

    pwr, pwi = k_ref[0, 8], k_ref[0, SCAN_K + 8]
    for t in range(n_tiles):
        rows = slice(8 * t, 8 * t + 8)
        cr = jnp.broadcast_to(c_ref[t:t + 1, :P2], (8, P2))
        ci = jnp.broadcast_to(c_ref[t:t + 1, P2:], (8, P2))
        e_ref[rows, :P2] = e_ref[rows, :P2] + pwr * cr - pwi * ci
        e_ref[rows, P2:] = e_ref[rows, P2:] + pwr * ci + pwi * cr

    y = jnp.dot(xb, wt_ref[0], preferred_element_type=F32)
    y = y + jnp.dot(e_ref[...].astype(BF16), we_ref[0], preferred_element_type=F32)
    y_ref[0] = y + x * d_ref[0]


def _ssm_scan(u, uc, w_state, w_toep, w_enter, consts, d_skip):
    bsz, L, _ = u.shape
    C = uc.shape[1]
    T, G, H = SSM_T, SSM_G, SSM_H
    nl, nc = L // T, C // T
    assert nl % 8 == 0 and nc % 8 == 0

    def to_chunks(v, n):
        return v.reshape(bsz, n, T, G, H).transpose(3, 0, 1, 2, 4).reshape(G, bsz * n, T * H)

    def from_chunks(v, n):
        return v.reshape(G, bsz, n, T, H).transpose(1, 2, 3, 0, 4).reshape(bsz, n * T, G * H)

    x = jnp.concatenate([to_chunks(u, nl), to_chunks(uc, nc)], axis=1)
    R = bsz * (nl + nc)
    n_tiles = R // 8
    d_tile = jnp.tile(d_skip.reshape(G, 1, H), (1, T, 1)).reshape(G, 1, T * H)
    gspec = lambda shape: pl.BlockSpec((1,) + shape, lambda g: (g,) + (0,) * len(shape))
    y = pl.pallas_call(
        functools.partial(_ssm_kernel, n_batch=bsz, lat_tiles=nl // 8, ctx_tiles=nc // 8),
        grid=(G,),
        in_specs=[gspec((R, T * H)), gspec((T * H, 4 * SSM_P)), gspec((T * H, T * H)),
                  gspec((4 * SSM_P, T * H)), gspec((2 * SCAN_K, 8, 2 * SSM_P)), gspec((1, T * H))],
        out_specs=gspec((R, T * H)),
        out_shape=jax.ShapeDtypeStruct((G, R, T * H), F32),
        scratch_shapes=[pltpu.VMEM((R, 4 * SSM_P), F32), pltpu.VMEM((R, 4 * SSM_P), F32),
                        pltpu.VMEM((n_tiles, 4 * SSM_P), F32), pltpu.VMEM((n_tiles, 4 * SSM_P), F32)],
        compiler_params=_params("arbitrary"),
        name="ssm_scan",
    )(x, w_state, w_toep, w_enter, consts, d_tile)
    return from_chunks(y[:, :bsz * nl], nl), from_chunks(y[:, bsz * nl:], nc)


def kernel(x, c, ctx, c_ctx, w_mod, b_mod, norm_g, ffn_wi, ffn_wo, w_in, w_out, attn_sink,
           ssm_lam_re, ssm_lam_im, ssm_log_dt, ssm_b_re, ssm_b_im, ssm_c_re, ssm_c_im, ssm_d,
           ssm_glu_w, ssm_glu_b, conv_w, conv_b, conv_ln_g, conv_ln_b):
    bsz, L, d = x.shape
    C = ctx.shape[1]
    depth = w_mod.shape[0]
    n_lat, n_ctx = bsz * L, bsz * C
    n_tok = n_lat + n_ctx
    n_pad = -(-n_tok // FFN_TM) * FFN_TM
    assert L % FFN_TM == 0 and n_tok % TOK_TM == 0 and n_lat % C == 0 and C % BLOCK == 0

    cond = jnp.concatenate([c, c_ctx[None, :], jnp.zeros((MOD_ROWS - bsz - 1, d), F32)], axis=0)
    mod_all = _modulation(cond, w_mod, b_mod)[:, :bsz + 1].reshape(depth, bsz + 1, 3, 3, d)

    w_ext = _extended_w_in(w_in)
    cos, sin = _rope_tables(L, n_pad - n_lat)
    cdft = _channel_dft()
    dft_l, dft_c = _position_dft(L), _position_dft(C)

    s = jnp.concatenate([x.reshape(n_lat, d), ctx.reshape(n_ctx, d), jnp.zeros((n_pad - n_tok, d), F32)], axis=0)
    for l in range(depth):
        need_ctx = l < depth - 1
        g = norm_g[l]
        m = mod_all[l]
        s = _ffn_sublayer(s, m[:, 0:1], g[0], g[1], ffn_wi[l, 0], ffn_wo[l, 0], L, n_pad)

        q, kd, vd, u, cg, xcs = _in_projection(s, m[:, 1:2], g[2], cos, sin, w_ext[l], cdft, L, n_lat, bsz)
        att = _attention(q, kd, vd, attn_sink[l], L, C, n_lat, bsz, need_ctx)
        ops = _ssm_operators(ssm_lam_re[l], ssm_lam_im[l], ssm_log_dt[l], ssm_b_re[l], ssm_b_im[l],
                             ssm_c_re[l], ssm_c_im[l])
        y_l, y_c = _ssm_scan(u[:n_lat].reshape(bsz, L, GROUP_W), u[n_lat:n_tok].reshape(bsz, C, GROUP_W),
                             *ops, ssm_d[l])
        y = jnp.concatenate([y_l.reshape(n_lat, GROUP_W), y_c.reshape(n_ctx, GROUP_W)], axis=0)
        cv = _conv_group(cg, conv_w[l], conv_b[l], conv_ln_g[l], conv_ln_b[l], L, C, n_lat, bsz, need_ctx)
        ff = _dft_call(dft_l, xcs, None, L, 0, bsz, "fourier_dft")
        if need_ctx:
            ff = _dft_call(dft_c, xcs, ff, C, n_lat, bsz, "fourier_dft_ctx")
        s = _out_projection(s, m[:, 1:2], g[3], att, y, cv, ff, ssm_glu_w[l], ssm_glu_b[l], w_out[l],
                            L, bsz, n_tok if need_ctx else n_lat)
        s = _ffn_sublayer(s, m[:, 2:3], g[4], g[5], ffn_wi[l, 1], ffn_wo[l, 1], L, n_pad if need_ctx else n_lat)
    return s[:n_lat].reshape(bsz, L, d)
```

```python
import functools
import math

import jax
import jax.numpy as jnp
from jax import lax
from jax.experimental import pallas as pl
from jax.experimental.pallas import tpu as pltpu

F32 = jnp.float32
BF16 = jnp.bfloat16

D_MODEL = 2048
DEPTH = 4
GRID_W = 64
N_MOD = 9
FFN_RES = 0.5
D_FF = 5632
RMS_EPS = 1e-6
LN_EPS = 1e-5
GROUP_W = 512
N_HEADS = 8
N_KV_HEADS = 2
HEAD_DIM = 64
WINDOW = 128
BLOCK = 128
ROPE_BASE = 10000.0
NEG_INF = -1e30
SSM_H = 16
SSM_G = 32
SSM_P = 64
LAMBDA_RE_MAX = -1e-4
SSM_T = 16
SCAN_K = 11
CONV_K = 31
CONV_HALO = 16
CONV_RB = 32
FFT_HEADS = 4
FFT_W = GROUP_W // FFT_HEADS
KV_W = N_KV_HEADS * HEAD_DIM
K_OFF = 0
V_OFF = K_OFF + KV_W
SSM_OFF = V_OFF + KV_W
Q_OFF = SSM_OFF + GROUP_W
CONV_OFF = Q_OFF + GROUP_W
FFT_OFF = CONV_OFF + 2 * GROUP_W
D_IN = FFT_OFF + GROUP_W

CONV_GATE_OFF = CONV_OFF + GROUP_W
ROPE_HALF = HEAD_DIM // 4

VMEM_LIMIT_BYTES = 56 * 1024 * 1024
LANES = 128
SUBLANES = 8
MOD_ROWS = 8
MOD_TN = 1024
FFN_TM = 1024
FFN_TF = 256
TOK_TM = 512
ROW_CHUNK = 16
DFT_TILE = 2048


FFN_VMEM_LIMIT_BYTES = 60000 * 1024


def _params(*sem, vmem=VMEM_LIMIT_BYTES):
    return pltpu.CompilerParams(dimension_semantics=sem, vmem_limit_bytes=vmem)


def _mod_kernel(c_ref, w_ref, b_ref, o_ref):
    c = c_ref[...]
    sc = (c * jax.nn.sigmoid(c)).astype(BF16)
    o_ref[0] = jnp.dot(sc, w_ref[0].astype(BF16), preferred_element_type=F32) + b_ref[0]


def _modulation(cond, w_mod, b_mod):
    depth, d, n = w_mod.shape
    return pl.pallas_call(
        _mod_kernel,
        grid=(depth, n // MOD_TN),
        in_specs=[
            pl.BlockSpec((MOD_ROWS, d), lambda l, j: (0, 0)),
            pl.BlockSpec((1, d, MOD_TN), lambda l, j: (l, 0, j)),
            pl.BlockSpec((1, 1, MOD_TN), lambda l, j: (l, 0, j)),
        ],
        out_specs=pl.BlockSpec((1, MOD_ROWS, MOD_TN), lambda l, j: (l, 0, j)),
        out_shape=jax.ShapeDtypeStruct((depth, MOD_ROWS, n), F32),
        compiler_params=_params("arbitrary", "arbitrary"),
        name="modulation",
    )(cond, w_mod, b_mod.reshape(depth, 1, n))


def _ffn_kernel(s_ref, mod_ref, gpre_ref, gpost_ref, wg_ref, wu_ref, wo_ref, o_ref, h_ref):
    j = pl.program_id(1)
    n_chunks = s_ref.shape[0] // ROW_CHUNK

    def row_chunks(body):
        for r in range(n_chunks):
            body(slice(r * ROW_CHUNK, (r + 1) * ROW_CHUNK))

    @pl.when(j == 0)
    def _():
        gain = gpre_ref[...] * (1.0 + mod_ref[0, 0, 1:2, :])
        shift = mod_ref[0, 0, 0:1, :]

        def body(rows):
            s = s_ref[rows, :]
            rs = lax.rsqrt(jnp.mean(s * s, axis=-1, keepdims=True) + RMS_EPS)
            h_ref[rows, :] = (s * rs * gain + shift).astype(BF16)
            o_ref[rows, :] = jnp.zeros((ROW_CHUNK, o_ref.shape[1]), F32)
        row_chunks(body)

    h = h_ref[...]
    gt = jnp.dot(h, wg_ref[...].astype(BF16), preferred_element_type=F32)
    up = jnp.dot(h, wu_ref[...].astype(BF16), preferred_element_type=F32)
    a = (gt * jax.nn.sigmoid(gt) * up).astype(BF16)
    o_ref[...] += jnp.dot(a, wo_ref[...].astype(BF16), preferred_element_type=F32)

    @pl.when(j == pl.num_programs(1) - 1)
    def _():
        gain = gpost_ref[...] * (FFN_RES * mod_ref[0, 0, 2:3, :])

        def body(rows):
            acc = o_ref[rows, :]
            rs = lax.rsqrt(jnp.mean(acc * acc, axis=-1, keepdims=True) + RMS_EPS)
            o_ref[rows, :] = s_ref[rows, :] + acc * rs * gain
        row_chunks(body)


def _ffn_sublayer(s, mod, g_pre, g_post, wi, wo, rows_per_mod, n_rows):
    d = s.shape[1]
    n_ff = wo.shape[0]
    tiles_per_mod = rows_per_mod // FFN_TM
    nj = n_ff // FFN_TF
    last_mod = mod.shape[0] - 1
    return pl.pallas_call(
        _ffn_kernel,
        grid=(n_rows // FFN_TM, nj),
        in_specs=[
            pl.BlockSpec((FFN_TM, d), lambda i, j: (i, 0)),
            pl.BlockSpec((1, 1, 3, d), lambda i, j: (jnp.minimum(i // tiles_per_mod, last_mod), 0, 0, 0)),
            pl.BlockSpec((1, d), lambda i, j: (0, 0)),
            pl.BlockSpec((1, d), lambda i, j: (0, 0)),
            pl.BlockSpec((d, FFN_TF), lambda i, j: (0, j)),
            pl.BlockSpec((d, FFN_TF), lambda i, j: (0, j + nj)),
            pl.BlockSpec((FFN_TF, d), lambda i, j: (j, 0)),
        ],
        out_specs=pl.BlockSpec((FFN_TM, d), lambda i, j: (i, 0)),
        out_shape=jax.ShapeDtypeStruct((n_rows, d), F32),
        scratch_shapes=[pltpu.VMEM((FFN_TM, d), BF16)],
        compiler_params=_params("arbitrary", "arbitrary", vmem=FFN_VMEM_LIMIT_BYTES),
        name="ffn_sublayer",
    )(s, mod, g_pre.reshape(1, d), g_post.reshape(1, d), wi, wi, wo)


def _rope_tables(L, n_tail):
    pos = jnp.arange(L)
    inv = ROPE_BASE ** (-jnp.arange(ROPE_HALF, dtype=F32) / ROPE_HALF)
    ang_r = (pos // GRID_W).astype(F32)[:, None] * inv[None, :]
    ang_c = (pos % GRID_W).astype(F32)[:, None] * inv[None, :]
    cos = jnp.concatenate([jnp.cos(ang_r), jnp.cos(ang_r), jnp.cos(ang_c), jnp.cos(ang_c)], axis=1)
    ssin = jnp.concatenate([-jnp.sin(ang_r), jnp.sin(ang_r), -jnp.sin(ang_c), jnp.sin(ang_c)], axis=1)
    cos = jnp.concatenate([jnp.tile(cos, (1, 4)), jnp.ones((n_tail, 4 * HEAD_DIM), F32)], axis=0)
    ssin = jnp.concatenate([jnp.tile(ssin, (1, 4)), jnp.zeros((n_tail, 4 * HEAD_DIM), F32)], axis=0)
    return cos, ssin


def _split_bf16(x):
    hi = x.astype(BF16)
    return hi, (x - hi.astype(F32)).astype(BF16)


def _channel_dft():
    c = jnp.arange(FFT_W)
    ang = (2.0 * math.pi / FFT_W) * ((c[:, None] * c[None, :]) % FFT_W).astype(F32)
    hi, lo = _split_bf16(jnp.concatenate([jnp.cos(ang), jnp.sin(ang)], axis=1))
    return jnp.stack([hi, lo])


def _position_dft(n, m=64):
    t = jnp.arange(n)

    def cs(k, period):
        ang = (2.0 * math.pi / period) * ((k[:, None] * t[None, :]) % period).astype(F32)
        return jnp.cos(ang), jnp.sin(ang)

    if n % m or n <= m * m // 4:
        c, s = cs(t, n)
    else:
        cb, sb = cs(jnp.arange(n // m), n // m)
        ca, sa = cs(jnp.arange(m), n)
        c = (cb[:, None] * ca[None] - sb[:, None] * sa[None]).reshape(n, n)
        s = (sb[:, None] * ca[None] + cb[:, None] * sa[None]).reshape(n, n)
    return jnp.concatenate([c, -s], axis=1).astype(BF16)


def _inproj_kernel(s_ref, mod_ref, g_ref, cos_ref, sin_ref, w_ref, cd_ref,
                   q_ref, kd_ref, vd_ref, u_ref, cg_ref, f_ref):
    s = s_ref[...]
    y = s * lax.rsqrt(jnp.mean(s * s, axis=-1, keepdims=True) + RMS_EPS) * g_ref[...]
    h = (y * (1.0 + mod_ref[0, 0, 1:2, :]) + mod_ref[0, 0, 0:1, :]).astype(BF16)

    def proj(lo, hi):
        return jnp.dot(h, w_ref[:, lo:hi], preferred_element_type=F32)

    def rope(x, cos, ssin):
        w = x.shape[1]
        lane = lax.broadcasted_iota(jnp.int32, (1, w), 1)
        first = (lane & (2 * ROPE_HALF - 1)) < ROPE_HALF
        partner = jnp.where(first, pltpu.roll(x, w - ROPE_HALF, axis=1), pltpu.roll(x, ROPE_HALF, axis=1))
        return x * cos + partner * ssin

    def both_halves(x):
        low = lax.broadcasted_iota(jnp.int32, (1, LANES), 1) < HEAD_DIM
        swapped = pltpu.roll(x, HEAD_DIM, axis=1)
        return jnp.concatenate([jnp.where(low, x, swapped), jnp.where(low, swapped, x)], axis=1)

    cos, ssin = cos_ref[...], sin_ref[...]
    w4 = 4 * HEAD_DIM
    for j in range(GROUP_W // w4):
        q = proj(Q_OFF + j * w4, Q_OFF + (j + 1) * w4)
        q_ref[:, j * w4:(j + 1) * w4] = (rope(q, cos, ssin) * HEAD_DIM ** -0.5).astype(BF16)
    k = rope(proj(K_OFF, V_OFF), cos[:, :KV_W], ssin[:, :KV_W])
    kd_ref[...] = both_halves(k).astype(BF16)
    vd_ref[...] = both_halves(proj(V_OFF, SSM_OFF)).astype(BF16)
    u_ref[...] = proj(SSM_OFF, Q_OFF)
    val = proj(CONV_OFF, CONV_GATE_OFF)
    cg_ref[...] = val * jax.nn.sigmoid(proj(CONV_GATE_OFF, FFT_OFF))
    f_hi, f_lo = _split_bf16(proj(FFT_OFF, D_IN))
    c_hi, c_lo = cd_ref[0], cd_ref[1]
    for hd in range(FFT_HEADS):
        sl = slice(hd * FFT_W, (hd + 1) * FFT_W)
        r = (jnp.dot(f_hi[:, sl], c_hi, preferred_element_type=F32)
             + jnp.dot(f_lo[:, sl], c_hi, preferred_element_type=F32)
             + jnp.dot(f_hi[:, sl], c_lo, preferred_element_type=F32))
        f_ref[0, :, sl] = r[:, :FFT_W].astype(BF16)
        f_ref[1, :, sl] = r[:, FFT_W:].astype(BF16)


def _in_projection(s, mod, g, cos, sin, w_in, cdft, L, n_lat, bsz):
    n_rows, d = s.shape
    tm = TOK_TM
    lat_tiles, seq_tiles = n_lat // tm, L // tm
    row = lambda i: (i, 0)
    tab = lambda i: (jnp.where(i < lat_tiles, i % seq_tiles, seq_tiles + i - lat_tiles), 0)
    const2 = lambda i: (0, 0)
    outs = [((n_rows, GROUP_W), BF16), ((n_rows, 2 * KV_W), BF16), ((n_rows, 2 * KV_W), BF16),
            ((n_rows, GROUP_W), F32), ((n_rows, GROUP_W), F32)]
    return pl.pallas_call(
        _inproj_kernel,
        grid=(n_rows // tm,),
        in_specs=[
            pl.BlockSpec((tm, d), row),
            pl.BlockSpec((1, 1, 3, d), lambda i: (jnp.minimum(i // seq_tiles, bsz), 0, 0, 0)),
            pl.BlockSpec((1, d), const2),
            pl.BlockSpec((tm, 4 * HEAD_DIM), tab),
            pl.BlockSpec((tm, 4 * HEAD_DIM), tab),
            pl.BlockSpec((d, D_IN), const2),
            pl.BlockSpec((2, FFT_W, 2 * FFT_W), lambda i: (0, 0, 0)),
        ],
        out_specs=[pl.BlockSpec((tm, shp[1]), row) for shp, _ in outs]
        + [pl.BlockSpec((2, tm, GROUP_W), lambda i: (0, i, 0))],
        out_shape=[jax.ShapeDtypeStruct(shp, dt) for shp, dt in outs]
        + [jax.ShapeDtypeStruct((2, n_rows, GROUP_W), BF16)],
        compiler_params=_params("arbitrary"),
        name="in_projection",
    )(s, mod, g.reshape(1, d), cos, sin, w_in, cdft)


def _attn_kernel(sink_ref, q_ref, *refs, local):
    if local:
        kp_ref, kc_ref, kn_ref, kx_ref, vp_ref, vc_ref, vn_ref, vx_ref, o_ref = refs
    else:
        kx_ref, vx_ref, o_ref = refs
    tq = q_ref.shape[0]
    grp = N_HEADS // N_KV_HEADS
    lane = lax.broadcasted_iota(jnp.int32, (1, LANES), 1)
    low = (lane < HEAD_DIM)
    half_mask = (low.astype(BF16), (~low).astype(BF16))
    rows = lax.broadcasted_iota(jnp.int32, (grp * tq, 1), 0)
    if local:
        n = pl.program_id(1)
        nk = 3 * BLOCK + kx_ref.shape[0]
        r = rows & (tq - 1)
        col = lax.broadcasted_iota(jnp.int32, (1, nk), 1)
        prev_ok = (col < BLOCK) & (r <= col) & (n > 0)
        next_ok = (col >= 2 * BLOCK) & (col < 3 * BLOCK) & (col - 2 * BLOCK <= r) & (n < pl.num_programs(1) - 1)
        always = ((col >= BLOCK) & (col < 2 * BLOCK)) | (col >= 3 * BLOCK)
        valid = prev_ok | next_ok | always
    for kh in range(N_KV_HEADS):
        ks = slice(kh * LANES, (kh + 1) * LANES)
        if local:
            k = jnp.concatenate([kp_ref[:, ks], kc_ref[:, ks], kn_ref[:, ks], kx_ref[:, ks]], axis=0)
            v = jnp.concatenate([vp_ref[:, ks], vc_ref[:, ks], vn_ref[:, ks], vx_ref[:, ks]], axis=0)
        else:
            k, v = kx_ref[:, ks], vx_ref[:, ks]
        qs = []
        sink = jnp.zeros((grp * tq, 1), F32)
        for hh in range(grp):
            hd = kh * grp + hh
            qs.append(q_ref[:, (hd // 2) * LANES:(hd // 2 + 1) * LANES] * half_mask[hd % 2])
            sink = jnp.where((rows >= hh * tq) & (rows < (hh + 1) * tq), sink_ref[hd], sink)
        q4 = jnp.concatenate(qs, axis=0)
        sc = lax.dot_general(q4, k, (((1,), (1,)), ((), ())), preferred_element_type=F32)
        if local:
            sc = jnp.where(valid, sc, NEG_INF)
        m = jnp.maximum(jnp.max(sc, axis=-1, keepdims=True), sink)
        p = jnp.exp(sc - m)
        denom = jnp.sum(p, axis=-1, keepdims=True) + jnp.exp(sink - m)
        o = jnp.dot(p.astype(BF16), v, preferred_element_type=F32) / denom
        for j in range(grp // 2):
            blk = jnp.where(low, o[(2 * j) * tq:(2 * j + 1) * tq], o[(2 * j + 1) * tq:(2 * j + 2) * tq])
            jb = kh * (grp // 2) + j
            o_ref[:, jb * LANES:(jb + 1) * LANES] = blk.astype(BF16)


def _attention(q, kd, vd, sink, L, C, n_lat, bsz, need_ctx):
    nb = L // BLOCK
    kw = 2 * KV_W
    smem = pl.BlockSpec(memory_space=pltpu.SMEM)
    blk = lambda f: pl.BlockSpec((BLOCK, kw), f)
    prev = lambda b, n: (b * nb + jnp.maximum(n - 1, 0), 0)
    cur = lambda b, n: (b * nb + n, 0)
    nxt = lambda b, n: (b * nb + jnp.minimum(n + 1, nb - 1), 0)
    cx = lambda b, n: (n_lat // C + b, 0)
    att = pl.pallas_call(
        functools.partial(_attn_kernel, local=True),
        grid=(bsz, nb),
        in_specs=[smem, pl.BlockSpec((BLOCK, GROUP_W), cur),
                  blk(prev), blk(cur), blk(nxt), pl.BlockSpec((C, kw), cx),
                  blk(prev), blk(cur), blk(nxt), pl.BlockSpec((C, kw), cx)],
        out_specs=pl.BlockSpec((BLOCK, GROUP_W), cur),
        out_shape=jax.ShapeDtypeStruct((n_lat, GROUP_W), BF16),
        compiler_params=_params("arbitrary", "arbitrary"),
        name="window_attention",
    )(sink, q, kd, kd, kd, kd, vd, vd, vd, vd)
    if not need_ctx:
        return att, None
    cb = lambda b: (n_lat // C + b, 0)
    att_c = pl.pallas_call(
        functools.partial(_attn_kernel, local=False),
        grid=(bsz,),
        in_specs=[smem, pl.BlockSpec((C, GROUP_W), cb), pl.BlockSpec((C, kw), cb), pl.BlockSpec((C, kw), cb)],
        out_specs=pl.BlockSpec((C, GROUP_W), lambda b: (b, 0)),
        out_shape=jax.ShapeDtypeStruct((bsz * C, GROUP_W), BF16),
        compiler_params=_params("arbitrary"),
        name="context_attention",
    )(sink, q, kd, vd)
    return att, att_c


def _conv_kernel(*refs, tiled):
    if tiled:
        xp_ref, xc_ref, xn_ref, w_ref, b_ref, g_ref, beta_ref, o_ref, ext_ref, sh_ref = refs
    else:
        xc_ref, w_ref, b_ref, g_ref, beta_ref, o_ref, ext_ref, sh_ref = refs
    tt = xc_ref.shape[0]
    H = CONV_HALO
    zeros = jnp.zeros((H, GROUP_W), F32)
    if tiled:
        n = pl.program_id(1)
        ext_ref[0:H, :] = jnp.where(n > 0, xp_ref[tt - H:tt, :], zeros)
        ext_ref[H + tt:2 * H + tt, :] = jnp.where(n < pl.num_programs(1) - 1, xn_ref[0:H, :], zeros)
    else:
        ext_ref[0:H, :] = zeros
        ext_ref[H + tt:2 * H + tt, :] = zeros
    ext_ref[H:H + tt, :] = xc_ref[...]
    n_sh = sh_ref.shape[1]
    for r in range(SUBLANES):
        sh_ref[r] = ext_ref[pl.ds(r, n_sh), :]
    off = H - CONV_K // 2
    for rb in range(tt // CONV_RB):
        acc = jnp.zeros((CONV_RB, GROUP_W), F32)
        for k in range(CONV_K):
            e = k + off
            tap = sh_ref[e % SUBLANES, pl.ds(rb * CONV_RB + (e // SUBLANES) * SUBLANES, CONV_RB), :]
            acc = acc + w_ref[k:k + 1, :] * tap
        acc = acc + b_ref[...]
        mu = jnp.mean(acc, axis=-1, keepdims=True)
        cen = acc - mu
        var = jnp.mean(cen * cen, axis=-1, keepdims=True)
        y = cen * lax.rsqrt(var + LN_EPS) * g_ref[...] + beta_ref[...]
        o_ref[rb * CONV_RB:(rb + 1) * CONV_RB, :] = (y * jax.nn.sigmoid(y)).astype(BF16)


def _conv_scratch(tt):
    n_sh = tt + (CONV_K + CONV_HALO - CONV_K // 2 - 1) // SUBLANES * SUBLANES
    return [pltpu.VMEM((tt + 2 * CONV_HALO, GROUP_W), F32), pltpu.VMEM((SUBLANES, n_sh, GROUP_W), F32)]


def _conv_group(cg, conv_w, conv_b, ln_g, ln_b, L, C, n_lat, bsz, need_ctx):
    tt = TOK_TM
    nt = L // tt
    vec = lambda a: a.reshape(1, GROUP_W)
    cvec = pl.BlockSpec((1, GROUP_W), lambda *_: (0, 0))
    wspec = pl.BlockSpec((CONV_K, GROUP_W), lambda *_: (0, 0))
    blk = lambda f: pl.BlockSpec((tt, GROUP_W), f)
    cur = lambda b, n: (b * nt + n, 0)
    out = pl.pallas_call(
        functools.partial(_conv_kernel, tiled=True),
        grid=(bsz, nt),
        in_specs=[blk(lambda b, n: (b * nt + jnp.maximum(n - 1, 0), 0)), blk(cur),
                  blk(lambda b, n: (b * nt + jnp.minimum(n + 1, nt - 1), 0)), wspec, cvec, cvec, cvec],
        out_specs=blk(cur),
        out_shape=jax.ShapeDtypeStruct((n_lat, GROUP_W), BF16),
        scratch_shapes=_conv_scratch(tt),
        compiler_params=_params("arbitrary", "arbitrary"),
        name="conv_group",
    )(cg, cg, cg, conv_w, vec(conv_b), vec(ln_g), vec(ln_b))
    if not need_ctx:
        return out, None
    out_c = pl.pallas_call(
        functools.partial(_conv_kernel, tiled=False),
        grid=(bsz,),
        in_specs=[pl.BlockSpec((C, GROUP_W), lambda b: (n_lat // C + b, 0)), wspec, cvec, cvec, cvec],
        out_specs=pl.BlockSpec((C, GROUP_W), lambda b: (b, 0)),
        out_shape=jax.ShapeDtypeStruct((bsz * C, GROUP_W), BF16),
        scratch_shapes=_conv_scratch(C),
        compiler_params=_params("arbitrary"),
        name="conv_group_ctx",
    )(cg, conv_w, vec(conv_b), vec(ln_g), vec(ln_b))
    return out, out_c


def _dft_kernel(d_ref, x_ref, o_ref, acc_ref, *, scale):
    kk = pl.program_id(2)

    @pl.when(kk == 0)
    def _():
        acc_ref[...] = jnp.zeros_like(acc_ref)

    acc_ref[...] += jnp.dot(d_ref[...], x_ref[0], preferred_element_type=F32)

    @pl.when(kk == pl.num_programs(2) - 1)
    def _():
        o_ref[...] = (acc_ref[...] * scale).astype(BF16)


def _dft_call(dmat, xcs, n, row0, bsz, name):
    t = min(n, DFT_TILE)
    nt = n // t
    return pl.pallas_call(
        functools.partial(_dft_kernel, scale=float((n * FFT_W) ** -0.5)),
        grid=(bsz, nt, 2 * nt),
        in_specs=[pl.BlockSpec((t, t), lambda b, m, kk: (m, kk)),
                  pl.BlockSpec((1, t, GROUP_W), lambda b, m, kk: (kk // nt, row0 // t + b * nt + kk % nt, 0))],
        out_specs=pl.BlockSpec((t, GROUP_W), lambda b, m, kk: (b * nt + m, 0)),
        out_shape=jax.ShapeDtypeStruct((bsz * n, GROUP_W), BF16),
        scratch_shapes=[pltpu.VMEM((t, GROUP_W), F32)],
        compiler_params=_params("arbitrary", "arbitrary", "arbitrary"),
        name=name,
    )(dmat, xcs)


def _outproj_kernel(s_ref, mod_ref, g_ref, y_ref, gw_ref, gb_ref, wo_ref, *refs, lat_tiles):
    o_ref = refs[-1]
    groups = [r[...] for r in refs[:3]]
    if len(refs) > 4:
        is_lat = pl.program_id(0) < lat_tiles
        groups = [jnp.where(is_lat, a, r[...]) for a, r in zip(groups, refs[3:6])]
    att, cv, ff = groups
    y = jax.nn.gelu(y_ref[...], approximate=True)
    z = jnp.dot(y.astype(BF16), gw_ref[...], preferred_element_type=F32) + gb_ref[...]
    ssm = (y * jax.nn.sigmoid(z)).astype(BF16)
    out = jnp.dot(att, wo_ref[0:GROUP_W, :], preferred_element_type=F32)
    out = out + jnp.dot(ssm, wo_ref[GROUP_W:2 * GROUP_W, :], preferred_element_type=F32)
    out = out + jnp.dot(cv, wo_ref[2 * GROUP_W:3 * GROUP_W, :], preferred_element_type=F32)
    out = out + jnp.dot(ff, wo_ref[3 * GROUP_W:4 * GROUP_W, :], preferred_element_type=F32)
    yn = out * lax.rsqrt(jnp.mean(out * out, axis=-1, keepdims=True) + RMS_EPS) * g_ref[...]
    o_ref[...] = s_ref[...] + mod_ref[0, 0, 2:3, :] * yn


def _out_projection(s, mod, g, y, glu_w, glu_b, w_out, lat, ctx, L, bsz, n_rows):
    d = s.shape[1]
    tm = TOK_TM
    seq_tiles = L // tm
    lat_tiles = lat[0].shape[0] // tm
    y_tiles = y.shape[0] // tm
    row = lambda i: (i, 0)
    const2 = lambda i: (0, 0)
    groups = [pl.BlockSpec((tm, GROUP_W), lambda i: (jnp.minimum(i, lat_tiles - 1), 0))] * 3
    args = list(lat)
    if ctx is not None:
        ctx_tiles = ctx[0].shape[0] // tm
        groups += [pl.BlockSpec((tm, GROUP_W), lambda i: (jnp.clip(i - lat_tiles, 0, ctx_tiles - 1), 0))] * 3
        args += list(ctx)
    return pl.pallas_call(
        functools.partial(_outproj_kernel, lat_tiles=lat_tiles),
        grid=(n_rows // tm,),
        in_specs=[pl.BlockSpec((tm, d), row),
                  pl.BlockSpec((1, 1, 3, d), lambda i: (jnp.minimum(i // seq_tiles, bsz), 0, 0, 0)),
                  pl.BlockSpec((1, d), const2),
                  pl.BlockSpec((tm, GROUP_W), lambda i: (jnp.minimum(i, y_tiles - 1), 0)),
                  pl.BlockSpec((GROUP_W, GROUP_W), const2), pl.BlockSpec((1, GROUP_W), const2),
                  pl.BlockSpec((4 * GROUP_W, d), const2)] + groups,
        out_specs=pl.BlockSpec((tm, d), row),
        out_shape=jax.ShapeDtypeStruct((n_rows, d), F32),
        compiler_params=_params("arbitrary"),
        name="out_projection",
    )(s, mod, g.reshape(1, d), y, glu_w, glu_b.reshape(1, GROUP_W), w_out, *args)


def _ssm_discretize(lam_re, lam_im, log_dt, b_re, b_im):
    lr = jnp.minimum(lam_re, LAMBDA_RE_MAX)
    li = lam_im
    dt = jnp.exp(log_dt)[..., None]
    mag = jnp.exp(lr * dt)
    a_re = mag * jnp.cos(li * dt)
    a_im = mag * jnp.sin(li * dt)
    den = lr * lr + li * li
    nr = a_re - 1.0
    coef_re = ((nr * lr + a_im * li) / den)[..., None]
    coef_im = ((a_im * lr - nr * li) / den)[..., None]
    return a_re, a_im, coef_re * b_re - coef_im * b_im, coef_re * b_im + coef_im * b_re


def _complex_powers(a_re, a_im, n):
    pr, pi = [jnp.ones_like(a_re)], [jnp.zeros_like(a_im)]
    for _ in range(n):
        pr.append(pr[-1] * a_re - pi[-1] * a_im)
        pi.append(pr[-2] * a_im + pi[-1] * a_re)
    return jnp.stack(pr), jnp.stack(pi)


def _ssm_operators(lam_re, lam_im, log_dt, b_re, b_im, c_re, c_im):
    T, G, P, H = SSM_T, SSM_G, SSM_P, SSM_H
    hp = lax.Precision.HIGHEST
    a_re, a_im, bb_re, bb_im = _ssm_discretize(lam_re, lam_im, log_dt, b_re, b_im)
    pr, pi = _complex_powers(a_re, a_im, T)
    ca_re = c_re[None] * pr[:T, :, :, None, :] - c_im[None] * pi[:T, :, :, None, :]
    ca_im = c_re[None] * pi[:T, :, :, None, :] + c_im[None] * pr[:T, :, :, None, :]
    kern = (jnp.einsum('tdgkp,dgph->tdgkh', ca_re, bb_re, precision=hp)
            - jnp.einsum('tdgkp,dgph->tdgkh', ca_im, bb_im, precision=hp))
    s_idx = jnp.arange(T)[:, None]
    t_idx = jnp.arange(T)[None, :]
    lag_f = t_idx - s_idx
    lag_b = s_idx - t_idx
    toep_f = jnp.where((lag_f >= 0)[:, :, None, None, None], kern[jnp.clip(lag_f, 0, T - 1), 0], 0.0)
    toep_b = jnp.where((lag_b >= 0)[:, :, None, None, None], kern[jnp.clip(lag_b, 0, T - 1), 1], 0.0)
    w_toep = (toep_f + toep_b).transpose(2, 0, 4, 1, 3).reshape(G, T * H, T * H)

    pw_re = jnp.stack([pr[:T, 0][::-1], pr[:T, 1]], axis=1)
    pw_im = jnp.stack([pi[:T, 0][::-1], pi[:T, 1]], axis=1)
    st_re = pw_re[..., None] * bb_re[None] - pw_im[..., None] * bb_im[None]
    st_im = pw_re[..., None] * bb_im[None] + pw_im[..., None] * bb_re[None]
    w_state = jnp.concatenate([st_re[:, 0], st_re[:, 1], st_im[:, 0], st_im[:, 1]], axis=2)
    w_state = w_state.transpose(1, 0, 3, 2).reshape(G, T * H, 4 * P)

    qw_re = jnp.stack([pr[1:, 0], pr[1:, 1][::-1]], axis=1)
    qw_im = jnp.stack([pi[1:, 0], pi[1:, 1][::-1]], axis=1)
    en_re = c_re[None] * qw_re[:, :, :, None, :] - c_im[None] * qw_im[:, :, :, None, :]
    en_im = -(c_re[None] * qw_im[:, :, :, None, :] + c_im[None] * qw_re[:, :, :, None, :])
    w_enter = jnp.concatenate([en_re[:, 0], en_re[:, 1], en_im[:, 0], en_im[:, 1]], axis=3)
    w_enter = w_enter.transpose(1, 3, 0, 2).reshape(G, 4 * P, T * H)

    qr, qi = _complex_powers(pr[T], pi[T], 8)
    r_idx = jnp.arange(8)[None, :]
    j_idx = jnp.arange(8)[:, None]

    def lanes(fwd_pow, bwd_pow, fwd_mask, bwd_mask):
        out = []
        for q in (qr, qi):
            f = jnp.where(fwd_mask[..., None, None], q[jnp.clip(fwd_pow, 0, 8), 0], 0.0)
            b = jnp.where(bwd_mask[..., None, None], q[jnp.clip(bwd_pow, 0, 8), 1], 0.0)
            out.append(jnp.concatenate([f, b], axis=-1))
        return out

    coef = lanes(r_idx - 1 - j_idx, j_idx - 1 - r_idx, r_idx > j_idx, r_idx < j_idx)
    rr = jnp.arange(8)
    ones = jnp.ones((8,), bool)
    powr = lanes(rr, 7 - rr, ones, ones)
    powt = lanes(7 - rr, rr, ones, ones)
    a8 = lanes(jnp.full((8,), 8), jnp.full((8,), 8), ones, ones)
    consts = jnp.concatenate(
        [jnp.concatenate([coef[i], powr[i][None], powt[i][None], a8[i][None]], axis=0) for i in range(2)],
        axis=0)
    consts = consts.transpose(2, 0, 1, 3)
    return w_state.astype(BF16), w_toep.astype(BF16), w_enter.astype(BF16), consts


def _ssm_kernel(x_ref, ws_ref, wt_ref, we_ref, k_ref, d_ref, y_ref, s_ref, e_ref, l_ref, c_ref,
                *, n_batch, lat_tiles, ctx_tiles):
    P2 = 2 * SSM_P
    n_tiles = n_batch * (lat_tiles + ctx_tiles)
    x = x_ref[0]
    xb = x.astype(BF16)
    s_ref[...] = jnp.dot(xb, ws_ref[0], preferred_element_type=F32)

    ptr, pti = k_ref[0, 9], k_ref[0, SCAN_K + 9]
    for t in range(n_tiles):
        rows = slice(8 * t, 8 * t + 8)
        a, b = s_ref[rows, :P2], s_ref[rows, P2:]
        er = jnp.zeros((8, P2), F32)
        ei = jnp.zeros((8, P2), F32)
        for j in range(8):
            cr, ci = k_ref[0, j], k_ref[0, SCAN_K + j]
            ar = jnp.broadcast_to(a[j:j + 1, :], (8, P2))
            ai = jnp.broadcast_to(b[j:j + 1, :], (8, P2))
            er = er + cr * ar - ci * ai
            ei = ei + cr * ai + ci * ar
        e_ref[rows, :P2] = er
        e_ref[rows, P2:] = ei
        l_ref[t:t + 1, :P2] = jnp.sum(ptr * a - pti * b, axis=0, keepdims=True)
        l_ref[t:t + 1, P2:] = jnp.sum(ptr * b + pti * a, axis=0, keepdims=True)

    a8r, a8i = k_ref[0, 10, 0:1, :], k_ref[0, SCAN_K + 10, 0:1, :]
    for bi in range(n_batch):
        lat = [bi * lat_tiles + i for i in range(lat_tiles)]
        ctx = [n_batch * lat_tiles + bi * ctx_tiles + i for i in range(ctx_tiles)]
        for order, lo in ((ctx + lat, 0), (ctx[::-1] + lat[::-1], SSM_P)):
            cr = jnp.zeros((1, P2), F32)
            ci = jnp.zeros((1, P2), F32)
            for t in order:
                c_ref[t:t + 1, lo:lo + SSM_P] = cr[:, lo:lo + SSM_P]
                c_ref[t:t + 1, P2 + lo:P2 + lo + SSM_P] = ci[:, lo:lo + SSM_P]
                lr, li = l_ref[t:t + 1, :P2], l_ref[t:t + 1, P2:]
                cr, ci = a8r * cr - a8i * ci + lr, a8r * ci + a8i * cr + li

    pwr, pwi = k_ref[0, 8], k_ref[0, SCAN_K + 8]
    for t in range(n_tiles):
        rows = slice(8 * t, 8 * t + 8)
        cr = jnp.broadcast_to(c_ref[t:t + 1, :P2], (8, P2))
        ci = jnp.broadcast_to(c_ref[t:t + 1, P2:], (8, P2))
        e_ref[rows, :P2] = e_ref[rows, :P2] + pwr * cr - pwi * ci
        e_ref[rows, P2:] = e_ref[rows, P2:] + pwr * ci + pwi * cr

    y = jnp.dot(xb, wt_ref[0], preferred_element_type=F32)
    y = y + jnp.dot(e_ref[...].astype(BF16), we_ref[0], preferred_element_type=F32)
    y_ref[0] = y + x * d_ref[0]


def _to_chunks_kernel(u_ref, x_ref):
    nch = u_ref.shape[0] // SSM_T
    steps = [u_ref[pl.ds(s, nch, stride=SSM_T), :] for s in range(SSM_T)]
    for g in range(LANES // SSM_H):
        x_ref[g] = jnp.concatenate([u[:, g * SSM_H:(g + 1) * SSM_H] for u in steps], axis=-1)


def _from_chunks_kernel(y_ref, o_ref):
    nch = o_ref.shape[0] // SSM_T
    for t in range(SSM_T):
        o_ref[pl.ds(t, nch, stride=SSM_T), :] = jnp.concatenate(
            [y_ref[g, :, t * SSM_H:(t + 1) * SSM_H] for g in range(LANES // SSM_H)], axis=-1)


def _chunk_layout_call(kernel_fn, v, n_tok, to_chunks, name):
    tm, gl = TOK_TM, LANES // SSM_H
    tok = pl.BlockSpec((tm, LANES), lambda i, c: (i, c))
    chk = pl.BlockSpec((gl, tm // SSM_T, SSM_T * SSM_H), lambda i, c: (c, i, 0))
    shape = (SSM_G, n_tok // SSM_T, SSM_T * SSM_H) if to_chunks else (n_tok, GROUP_W)
    return pl.pallas_call(
        kernel_fn, grid=(n_tok // tm, GROUP_W // LANES),
        in_specs=[tok if to_chunks else chk], out_specs=chk if to_chunks else tok,
        out_shape=jax.ShapeDtypeStruct(shape, F32),
        compiler_params=_params("arbitrary", "arbitrary"), name=name,
    )(v)


def _ssm_scan(u, w_state, w_toep, w_enter, consts, d_skip, bsz, L, C):
    T, G, H = SSM_T, SSM_G, SSM_H
    nl, nc = L // T, C // T
    assert nl % 8 == 0 and nc % 8 == 0
    n_tok = bsz * (L + C)
    x = _chunk_layout_call(_to_chunks_kernel, u, n_tok, True, "ssm_to_chunks")
    R = bsz * (nl + nc)
    n_tiles = R // 8
    d_tile = jnp.tile(d_skip.reshape(G, 1, H), (1, T, 1)).reshape(G, 1, T * H)
    gspec = lambda shape: pl.BlockSpec((1,) + shape, lambda g: (g,) + (0,) * len(shape))
    y = pl.pallas_call(
        functools.partial(_ssm_kernel, n_batch=bsz, lat_tiles=nl // 8, ctx_tiles=nc // 8),
        grid=(G,),
        in_specs=[gspec((R, T * H)), gspec((T * H, 4 * SSM_P)), gspec((T * H, T * H)),
                  gspec((4 * SSM_P, T * H)), gspec((2 * SCAN_K, 8, 2 * SSM_P)), gspec((1, T * H))],
        out_specs=gspec((R, T * H)),
        out_shape=jax.ShapeDtypeStruct((G, R, T * H), F32),
        scratch_shapes=[pltpu.VMEM((R, 4 * SSM_P), F32), pltpu.VMEM((R, 4 * SSM_P), F32),
                        pltpu.VMEM((n_tiles, 4 * SSM_P), F32), pltpu.VMEM((n_tiles, 4 * SSM_P), F32)],
        compiler_params=_params("arbitrary"),
        name="ssm_scan",
    )(x, w_state, w_toep, w_enter, consts, d_tile)
    return _chunk_layout_call(_from_chunks_kernel, y, n_tok, False, "ssm_from_chunks")


def kernel(x, c, ctx, c_ctx, w_mod, b_mod, norm_g, ffn_wi, ffn_wo, w_in, w_out, attn_sink,
           ssm_lam_re, ssm_lam_im, ssm_log_dt, ssm_b_re, ssm_b_im, ssm_c_re, ssm_c_im, ssm_d,
           ssm_glu_w, ssm_glu_b, conv_w, conv_b, conv_ln_g, conv_ln_b):
    bsz, L, d = x.shape
    C = ctx.shape[1]
    depth = w_mod.shape[0]
    n_lat, n_ctx = bsz * L, bsz * C
    n_tok = n_lat + n_ctx
    n_pad = -(-n_tok // FFN_TM) * FFN_TM
    assert L % FFN_TM == 0 and n_tok % TOK_TM == 0 and n_lat % C == 0 and C % BLOCK == 0

    cond = jnp.concatenate([c, c_ctx[None, :], jnp.zeros((MOD_ROWS - bsz - 1, d), F32)], axis=0)
    mod_all = _modulation(cond, w_mod, b_mod)[:, :bsz + 1].reshape(depth, bsz + 1, 3, 3, d)

    w_in_b, w_out_b, glu_w_b = w_in.astype(BF16), w_out.astype(BF16), ssm_glu_w.astype(BF16)
    ssm_ops = jax.vmap(_ssm_operators)(ssm_lam_re, ssm_lam_im, ssm_log_dt, ssm_b_re, ssm_b_im, ssm_c_re, ssm_c_im)
    cos, sin = _rope_tables(L, n_pad - n_lat)
    cdft = _channel_dft()
    dft_l, dft_c = _position_dft(L), _position_dft(C)

    s = jnp.concatenate([x.reshape(n_lat, d), ctx.reshape(n_ctx, d), jnp.zeros((n_pad - n_tok, d), F32)], axis=0)
    for l in range(depth):
        need_ctx = l < depth - 1
        g = norm_g[l]
        m = mod_all[l]
        s = _ffn_sublayer(s, m[:, 0:1], g[0], g[1], ffn_wi[l, 0], ffn_wo[l, 0], L, n_pad)

        q, kd, vd, u, cg, xcs = _in_projection(s, m[:, 1:2], g[2], cos, sin, w_in_b[l], cdft, L, n_lat, bsz)
        att, att_c = _attention(q, kd, vd, attn_sink[l], L, C, n_lat, bsz, need_ctx)
        y = _ssm_scan(u, *[op[l] for op in ssm_ops], ssm_d[l], bsz, L, C)
        cv, cv_c = _conv_group(cg, conv_w[l], conv_b[l], conv_ln_g[l], conv_ln_b[l], L, C, n_lat, bsz, need_ctx)
        ff = _dft_call(dft_l, xcs, L, 0, bsz, "fourier_dft")
        ff_c = _dft_call(dft_c, xcs, C, n_lat, bsz, "fourier_dft_ctx") if need_ctx else None
        n_out = n_pad if need_ctx else n_lat
        s = _out_projection(s, m[:, 1:2], g[3], y, glu_w_b[l], ssm_glu_b[l], w_out_b[l], (att, cv, ff),
                            (att_c, cv_c, ff_c) if need_ctx else None, L, bsz, n_out)
        s = _ffn_sublayer(s, m[:, 2:3], g[4], g[5], ffn_wi[l, 1], ffn_wo[l, 1], L, n_out)
    return s[:n_lat].reshape(bsz, L, d)
```
